```python
import jax, jax.numpy as jnp
from jax import lax
import numpy as np

D_MODEL = 2048
BATCH = 2
SEQ = 8192
DEPTH = 4

GRID_W = 64
CTX_LEN = 256
HEAD_DIM = 128
A_HEADS = 8
A_KV_HEADS = 2
B_HEADS = 8
B_KV_HEADS = 2
C_HEADS = 8
C_QK_DIM = HEAD_DIM // 2
BRANCH_WIDTH = 1024
N_BRANCH = 3
WINDOW = 128
Q_BLOCK = 128
D_FF = 5632
HALF_STEP = 0.5
ROPE_THETA = 10000.0
NORM_EPS = 1e-6
NEG_INF = -1e30
N_MOD = 9
IN_SPLITS = (A_HEADS * HEAD_DIM, A_KV_HEADS * HEAD_DIM, A_KV_HEADS * HEAD_DIM,
             B_HEADS * HEAD_DIM, B_KV_HEADS * HEAD_DIM, B_KV_HEADS * HEAD_DIM,
             C_HEADS * HEAD_DIM, C_HEADS * HEAD_DIM, C_HEADS * HEAD_DIM,
             N_BRANCH * D_MODEL)
IN_WIDTH = (2 * A_KV_HEADS + A_HEADS) * HEAD_DIM + (2 * B_KV_HEADS + B_HEADS) * HEAD_DIM + 3 * C_HEADS * HEAD_DIM + N_BRANCH * D_MODEL

kernel_name = "hybrid_gated_attn_diffusion_trunk"


def rms_norm(x, g):
    xf = x.astype(jnp.float32)
    y = xf * lax.rsqrt(jnp.mean(xf * xf, axis=-1, keepdims=True) + NORM_EPS)
    return (y * g.astype(jnp.float32)).astype(x.dtype)


def ada_norm(x, g, shift, scale):
    return rms_norm(x, g) * (1 + scale) + shift


def swiglu(h, w1, w2):
    gate, up = jnp.split(h @ w1, 2, axis=-1)
    return (jax.nn.silu(gate) * up) @ w2


def split_columns(p):
    return jnp.split(p, np.cumsum(IN_SPLITS)[:-1].tolist(), axis=-1)


def split_heads(t, n_heads):
    b, l, _ = t.shape
    return t.reshape(b, l, n_heads, -1).transpose(0, 2, 1, 3)


def merge_heads(t):
    b, h, l, d = t.shape
    return t.transpose(0, 2, 1, 3).reshape(b, l, h * d)


def group_queries(q, n_kv):
    b, h, l, d = q.shape
    return q.reshape(b, n_kv, h // n_kv, l, d)


def ungroup(o):
    b, k, g, l, d = o.shape
    return o.transpose(0, 3, 1, 2, 4).reshape(b, l, k * g * d)


def to_blocks(q):
    nb = q.shape[-2] // Q_BLOCK
    return jnp.moveaxis(q.reshape(*q.shape[:-2], nb, Q_BLOCK, q.shape[-1]), -3, 0)


def from_blocks(o):
    o = jnp.moveaxis(o, 0, -3)
    return o.reshape(*o.shape[:-3], o.shape[-3] * o.shape[-2], o.shape[-1])


def axial_rope_tables(n_tok, rot_dim):
    rows = n_tok // GRID_W
    row = jnp.repeat(jnp.arange(rows, dtype=jnp.int32), GRID_W).astype(jnp.float32)
    col = jnp.tile(jnp.arange(GRID_W, dtype=jnp.int32), rows).astype(jnp.float32)
    d_ax = rot_dim // 2
    inv = ROPE_THETA ** (-jnp.arange(0, d_ax, 2, dtype=jnp.float32) / d_ax)
    ang_r = row[:, None] * inv[None, :]
    ang_c = col[:, None] * inv[None, :]
    ang = jnp.concatenate([ang_r, ang_r, ang_c, ang_c], axis=-1)
    return jnp.cos(ang), jnp.sin(ang)


def apply_rope(x, tables):
    cos, sin = tables
    a, b, c, d = jnp.split(x, 4, axis=-1)
    rot = jnp.concatenate([-b, a, -d, c], axis=-1)
    return x * cos.astype(x.dtype) + rot * sin.astype(x.dtype)


def softmax_attend(q, k, v, scale):
    s = jnp.einsum('bkgqd,bksd->bkgqs', q, k, preferred_element_type=jnp.float32) * scale
    p = jax.nn.softmax(s, axis=-1).astype(v.dtype)
    return jnp.einsum('bkgqs,bksd->bkgqd', p, v)


def sink_softmax_attend(q, k, v, scale, sink, mask):
    s = jnp.einsum('bkgqd,bksd->bkgqs', q, k, preferred_element_type=jnp.float32) * scale
    if mask is not None:
        s = jnp.where(mask, s, NEG_INF)
    m = jnp.maximum(jnp.max(s, axis=-1, keepdims=True), sink)
    e = jnp.exp(s - m)
    p = e / (jnp.sum(e, axis=-1, keepdims=True) + jnp.exp(sink - m))
    return jnp.einsum('bkgqs,bksd->bkgqd', p.astype(v.dtype), v)


def diff_attend(q1, q2, k1, k2, v, lam, scale):
    s1 = jnp.einsum('bhqd,bhsd->bhqs', q1, k1, preferred_element_type=jnp.float32) * scale
    s2 = jnp.einsum('bhqd,bhsd->bhqs', q2, k2, preferred_element_type=jnp.float32) * scale
    p = jax.nn.softmax(s1, axis=-1) - lam * jax.nn.softmax(s2, axis=-1)
    return jnp.einsum('bhqs,bhsd->bhqd', p.astype(v.dtype), v)


def branch_global(pc, pl, qk_norm, rope, ctx_out):
    def prep(p, use_rope):
        q = rms_norm(split_heads(p[0], A_HEADS), qk_norm[0])
        k = rms_norm(split_heads(p[1], A_KV_HEADS), qk_norm[1])
        v = split_heads(p[2], A_KV_HEADS)
        if use_rope:
            q, k = apply_rope(q, rope), apply_rope(k, rope)
        return group_queries(q, A_KV_HEADS), k, v
    scale = HEAD_DIM ** -0.5
    qc, kc, vc = prep(pc, False)
    ql, kl, vl = prep(pl, True)
    k_all = jnp.concatenate([kc, kl], axis=2)
    v_all = jnp.concatenate([vc, vl], axis=2)
    o_lat = from_blocks(lax.map(lambda qb: softmax_attend(qb, k_all, v_all, scale), to_blocks(ql)))
    o_ctx = ungroup(softmax_attend(qc, kc, vc, scale)) if ctx_out else None
    return o_ctx, ungroup(o_lat)


def branch_window(pc, pl, sink, rope, ctx_out):
    scale = HEAD_DIM ** -0.5
    qc = group_queries(split_heads(pc[0], B_HEADS), B_KV_HEADS)
    kc, vc = split_heads(pc[1], B_KV_HEADS), split_heads(pc[2], B_KV_HEADS)
    ql = group_queries(apply_rope(split_heads(pl[0], B_HEADS), rope), B_KV_HEADS)
    kl = apply_rope(split_heads(pl[1], B_KV_HEADS), rope)
    vl = split_heads(pl[2], B_KV_HEADS)
    sink_g = sink.astype(jnp.float32).reshape(B_KV_HEADS, -1)[None, :, :, None, None]
    n = kl.shape[2]
    nb = n // Q_BLOCK
    span = Q_BLOCK + 2 * WINDOW
    pad = ((0, 0), (0, 0), (WINDOW, WINDOW), (0, 0))
    kp, vp = jnp.pad(kl, pad), jnp.pad(vl, pad)
    rel = jnp.arange(span)[None, :] - WINDOW - jnp.arange(Q_BLOCK)[:, None]
    in_band = jnp.abs(rel) <= WINDOW
    ctx_ok = jnp.ones((Q_BLOCK, kc.shape[2]), dtype=bool)

    def one(args):
        start, qb = args
        kb = lax.dynamic_slice_in_dim(kp, start, span, axis=2)
        vb = lax.dynamic_slice_in_dim(vp, start, span, axis=2)
        kpos = start - WINDOW + jnp.arange(span)
        ok = in_band & ((kpos >= 0) & (kpos < n))[None, :]
        mask = jnp.concatenate([ctx_ok, ok], axis=1)
        return sink_softmax_attend(qb, jnp.concatenate([kc, kb], axis=2),
                                   jnp.concatenate([vc, vb], axis=2), scale, sink_g, mask)

    starts = jnp.arange(nb, dtype=jnp.int32) * Q_BLOCK
    o_lat = from_blocks(lax.map(one, (starts, to_blocks(ql))))
    o_ctx = ungroup(sink_softmax_attend(qc, kc, vc, scale, sink_g, None)) if ctx_out else None
    return o_ctx, ungroup(o_lat)


def branch_diff(pc, pl, lam, subln, lambda_init, rope, ctx_out):
    def prep(p, use_rope):
        q, k, v = split_heads(p[0], C_HEADS), split_heads(p[1], C_HEADS), split_heads(p[2], C_HEADS)
        q1, q2 = jnp.split(q, 2, axis=-1)
        k1, k2 = jnp.split(k, 2, axis=-1)
        if use_rope:
            q1, q2, k1, k2 = (apply_rope(t, rope) for t in (q1, q2, k1, k2))
        return q1, q2, k1, k2, v
    scale = C_QK_DIM ** -0.5
    lf = lam.astype(jnp.float32)
    lam_val = jnp.exp(jnp.sum(lf[0] * lf[1])) - jnp.exp(jnp.sum(lf[2] * lf[3])) + lambda_init
    q1c, q2c, k1c, k2c, vc = prep(pc, False)
    q1l, q2l, k1l, k2l, vl = prep(pl, True)
    k1a = jnp.concatenate([k1c, k1l], axis=2)
    k2a = jnp.concatenate([k2c, k2l], axis=2)
    va = jnp.concatenate([vc, vl], axis=2)
    o_lat = from_blocks(lax.map(lambda a: diff_attend(a[0], a[1], k1a, k2a, va, lam_val, scale),
                                (to_blocks(q1l), to_blocks(q2l))))
    post = lambda o: merge_heads(rms_norm(o, subln) * (1 - lambda_init))
    o_ctx = post(diff_attend(q1c, q2c, k1c, k2c, vc, lam_val, scale)) if ctx_out else None
    return o_ctx, post(o_lat)


def merge_branches(outs, gate_cols, w_branch, w_out):
    gates = jnp.split(jax.nn.sigmoid(gate_cols), N_BRANCH, axis=-1)
    y = gates[0] * (outs[0] @ w_branch[0])
    for r in range(1, N_BRANCH):
        y = y + gates[r] * (outs[r] @ w_branch[r])
    return y @ w_out


def token_mixer(h_ctx, h_lat, w_in, qk_norm_a, sink_b, lam_c, subln_c, w_branch, w_out,
                lambda_init, rope_ab, rope_c, ctx_out):
    pc = split_columns(h_ctx @ w_in)
    pl = split_columns(h_lat @ w_in)
    a_ctx, a_lat = branch_global(pc[0:3], pl[0:3], qk_norm_a, rope_ab, ctx_out)
    b_ctx, b_lat = branch_window(pc[3:6], pl[3:6], sink_b, rope_ab, ctx_out)
    d_ctx, d_lat = branch_diff(pc[6:9], pl[6:9], lam_c, subln_c, lambda_init, rope_c, ctx_out)
    y_lat = merge_branches((a_lat, b_lat, d_lat), pl[9], w_branch, w_out)
    y_ctx = merge_branches((a_ctx, b_ctx, d_ctx), pc[9], w_branch, w_out) if ctx_out else None
    return y_ctx, y_lat


def ffn_half_step(x, mod, g_pre, g_post, w1, w2):
    shift, scale, gate = mod
    h = ada_norm(x, g_pre, shift, scale)
    return x + HALF_STEP * gate * rms_norm(swiglu(h, w1, w2), g_post)


def setup_inputs(seed: int = 0) -> dict:
    key = jax.random.key(seed)
    ks = jax.random.split(key, 17)
    f32 = jnp.float32

    def normal(k, shape, std):
        return jax.random.normal(k, shape, f32) * std

    return {
        "x": normal(ks[0], (BATCH, SEQ, D_MODEL), 1.0),
        "c": normal(ks[1], (BATCH, D_MODEL), 1.0),
        "ctx": normal(ks[2], (BATCH, CTX_LEN, D_MODEL), 1.0),
        "c_ctx": normal(ks[3], (D_MODEL,), 1.0),
        "w_ada": normal(ks[4], (DEPTH, D_MODEL, N_MOD * D_MODEL), 0.5 * D_MODEL ** -0.5),
        "b_ada": normal(ks[5], (DEPTH, N_MOD * D_MODEL), 0.01),
        "norm_pre": 1.0 + normal(ks[6], (DEPTH, 3, D_MODEL), 0.02),
        "norm_post": 1.0 + normal(ks[7], (DEPTH, 3, D_MODEL), 0.02),
        "w_ff_in": normal(ks[8], (DEPTH, 2, D_MODEL, 2 * D_FF), D_MODEL ** -0.5),
        "w_ff_out": normal(ks[9], (DEPTH, 2, D_FF, D_MODEL), D_FF ** -0.5),
        "w_in": normal(ks[10], (DEPTH, D_MODEL, IN_WIDTH), D_MODEL ** -0.5),
        "qk_norm_a": 1.0 + normal(ks[11], (DEPTH, 2, HEAD_DIM), 0.02),
        "sink_b": normal(ks[12], (DEPTH, B_HEADS), 0.5),
        "lam_c": normal(ks[13], (DEPTH, 4, C_QK_DIM), 0.1),
        "subln_c": 1.0 + normal(ks[14], (DEPTH, HEAD_DIM), 0.02),
        "w_branch": normal(ks[15], (DEPTH, N_BRANCH, BRANCH_WIDTH, D_MODEL), BRANCH_WIDTH ** -0.5),
        "w_out": normal(ks[16], (DEPTH, D_MODEL, D_MODEL), D_MODEL ** -0.5),
    }


def reference(x, c, ctx, c_ctx, w_ada, b_ada, norm_pre, norm_post, w_ff_in, w_ff_out, w_in,
              qk_norm_a, sink_b, lam_c, subln_c, w_branch, w_out):
    n_lat = x.shape[1]
    rope_ab = axial_rope_tables(n_lat, HEAD_DIM)
    rope_c = axial_rope_tables(n_lat, C_QK_DIM)
    x_lat, x_ctx = x, ctx
    for l in range(DEPTH):
        last = l == DEPTH - 1
        lambda_init = 0.8 - 0.6 * float(np.exp(-0.3 * l))
        mod_lat = jnp.split((jax.nn.silu(c) @ w_ada[l] + b_ada[l])[:, None, :], N_MOD, axis=-1)
        mod_ctx = jnp.split((jax.nn.silu(c_ctx) @ w_ada[l] + b_ada[l])[None, None, :], N_MOD, axis=-1)
        x_ctx = ffn_half_step(x_ctx, mod_ctx[0:3], norm_pre[l, 0], norm_post[l, 0], w_ff_in[l, 0], w_ff_out[l, 0])
        x_lat = ffn_half_step(x_lat, mod_lat[0:3], norm_pre[l, 0], norm_post[l, 0], w_ff_in[l, 0], w_ff_out[l, 0])
        h_ctx = ada_norm(x_ctx, norm_pre[l, 1], mod_ctx[3], mod_ctx[4])
        h_lat = ada_norm(x_lat, norm_pre[l, 1], mod_lat[3], mod_lat[4])
        y_ctx, y_lat = token_mixer(h_ctx, h_lat, w_in[l], qk_norm_a[l], sink_b[l], lam_c[l], subln_c[l],
                                   w_branch[l], w_out[l], lambda_init, rope_ab, rope_c, not last)
        x_lat = x_lat + mod_lat[5] * rms_norm(y_lat, norm_post[l, 1])
        x_lat = ffn_half_step(x_lat, mod_lat[6:9], norm_pre[l, 2], norm_post[l, 2], w_ff_in[l, 1], w_ff_out[l, 1])
        if not last:
            x_ctx = x_ctx + mod_ctx[5] * rms_norm(y_ctx, norm_post[l, 1])
            x_ctx = ffn_half_step(x_ctx, mod_ctx[6:9], norm_pre[l, 2], norm_post[l, 2], w_ff_in[l, 1], w_ff_out[l, 1])
    return x_lat
```

```python
import functools
import math

import jax
import jax.numpy as jnp
from jax import lax
from jax.experimental import pallas as pl
from jax.experimental.pallas import tpu as pltpu

D_MODEL = 2048
DEPTH = 4
GRID_W = 64
HEAD_DIM = 128
A_HEADS = 8
A_KV_HEADS = 2
B_HEADS = 8
B_KV_HEADS = 2
C_HEADS = 8
C_QK_DIM = HEAD_DIM // 2
BRANCH_WIDTH = 1024
N_BRANCH = 3
WINDOW = 128
D_FF = 5632
HALF_STEP = 0.5
ROPE_THETA = 10000.0
NORM_EPS = 1e-6
NEG_INF = -1e30
N_MOD = 9
QKV_WIDTH = 6144
LANES = 128
SUBLANES = 8
VMEM_LIMIT = 56 * 1024 * 1024

A_Q, A_K, A_V = 0, 8, 10
B_Q, B_K, B_V = 12, 20, 22
C_Q, C_K, C_V = 24, 32, 40

F32 = jnp.float32
BF16 = jnp.bfloat16


def _params(sem, vmem=VMEM_LIMIT):
    return pltpu.CompilerParams(dimension_semantics=sem, vmem_limit_bytes=vmem)


def _rms(x, g):
    ms = jnp.mean(x * x, axis=-1, keepdims=True)
    return x * lax.rsqrt(ms + NORM_EPS) * g


def _dot(a, b):
    return jnp.dot(a, b, preferred_element_type=F32)


def _dot_nt(a, b):
    return lax.dot_general(a, b, (((1,), (1,)), ((), ())), preferred_element_type=F32)


MOD_TN = 1024


def _mod_kernel(c_ref, w_ref, b_ref, o_ref):
    n_rep = MOD_TN // LANES

    def body(kc, accs):
        r0 = pl.multiple_of(kc * SUBLANES, SUBLANES)
        wk = w_ref[0, pl.ds(r0, SUBLANES), :]
        out = []
        for m in range(3):
            cm = c_ref[m, pl.ds(r0, SUBLANES), :]
            sm = cm * jax.nn.sigmoid(cm)
            out.append(accs[m] + jnp.concatenate([sm] * n_rep, axis=1) * wk)
        return tuple(out)

    zero = jnp.zeros((SUBLANES, MOD_TN), F32)
    accs = lax.fori_loop(0, D_MODEL // SUBLANES, body, (zero, zero, zero), unroll=4)
    bias = b_ref[0]
    for m in range(3):
        o_ref[0, m:m + 1, :] = jnp.sum(accs[m], axis=0, keepdims=True) + bias
    o_ref[0, 3:SUBLANES, :] = jnp.zeros((SUBLANES - 3, MOD_TN), F32)


def _modulation(cvec, w_ada, b_ada):
    width = N_MOD * D_MODEL
    c_b = jnp.broadcast_to(cvec[:, :, None], (3, D_MODEL, LANES))
    return pl.pallas_call(
        _mod_kernel,
        grid=(DEPTH, width // MOD_TN),
        in_specs=[
            pl.BlockSpec((3, D_MODEL, LANES), lambda l, j: (0, 0, 0)),
            pl.BlockSpec((1, D_MODEL, MOD_TN), lambda l, j: (l, 0, j)),
            pl.BlockSpec((1, 1, MOD_TN), lambda l, j: (l, 0, j)),
        ],
        out_specs=pl.BlockSpec((1, SUBLANES, MOD_TN), lambda l, j: (l, 0, j)),
        out_shape=jax.ShapeDtypeStruct((DEPTH, SUBLANES, width), F32),
        name="adaln_modulation",
        compiler_params=_params(("arbitrary", "arbitrary")),
    )(c_b, w_ada, b_ada.reshape(DEPTH, 1, width))


FFN_TF = 512


def _ffn_kernel(x_ref, mod_ref, gpre_ref, gpost_ref, w1g_ref, w1u_ref, w2_ref, o_ref,
                h_ref, acc_ref, *, mod_row):
    j = pl.program_id(1)

    @pl.when(j == 0)
    def _():
        shift = mod_ref[mod_row:mod_row + 1, :]
        scale = mod_ref[mod_row + 1:mod_row + 2, :]
        h = _rms(x_ref[...], gpre_ref[...]) * (1.0 + scale) + shift
        h_ref[...] = h.astype(BF16)

    h = h_ref[...]
    g = _dot(h, w1g_ref[...])
    u = _dot(h, w1u_ref[...])
    a = (g * jax.nn.sigmoid(g)) * u
    part = _dot(a.astype(BF16), w2_ref[...])

    @pl.when(j == 0)
    def _():
        acc_ref[...] = part

    @pl.when(j > 0)
    def _():
        acc_ref[...] += part

    @pl.when(j == pl.num_programs(1) - 1)
    def _():
        gate = mod_ref[mod_row + 2:mod_row + 3, :]
        y = _rms(acc_ref[...], gpost_ref[...])
        o_ref[...] = x_ref[...] + HALF_STEP * gate * y


def _ffn_half_step(x, mod, g_pre, g_post, w1, w2, *, mod_row, tm, rows_per_group):
    rows = x.shape[0]
    nf = D_FF // FFN_TF
    tiles_per_group = rows_per_group // tm
    return pl.pallas_call(
        functools.partial(_ffn_kernel, mod_row=mod_row),
        grid=(rows // tm, nf),
        in_specs=[
            pl.BlockSpec((tm, D_MODEL), lambda i, j: (i, 0)),
            pl.BlockSpec((None, N_MOD, D_MODEL), lambda i, j: (i // tiles_per_group, 0, 0)),
            pl.BlockSpec((1, D_MODEL), lambda i, j: (0, 0)),
            pl.BlockSpec((1, D_MODEL), lambda i, j: (0, 0)),
            pl.BlockSpec((D_MODEL, FFN_TF), lambda i, j: (0, j)),
            pl.BlockSpec((D_MODEL, FFN_TF), lambda i, j: (0, nf + j)),
            pl.BlockSpec((FFN_TF, D_MODEL), lambda i, j: (j, 0)),
        ],
        out_specs=pl.BlockSpec((tm, D_MODEL), lambda i, j: (i, 0)),
        out_shape=jax.ShapeDtypeStruct((rows, D_MODEL), F32),
        scratch_shapes=[pltpu.VMEM((tm, D_MODEL), BF16), pltpu.VMEM((tm, D_MODEL), F32)],
        name="ffn_half_step",
        compiler_params=_params(("arbitrary", "arbitrary")),
    )(x, mod, g_pre, g_post, w1, w1, w2)


PROJ_TN = 512
PROJ_CHUNKS = PROJ_TN // LANES


def _rope(y, tab_ref, base, quarter):
    cos, sin_up, sin_dn = tab_ref[base], tab_ref[base + 1], tab_ref[base + 2]
    return (y * cos + pltpu.roll(y, LANES - quarter, 1) * sin_up + pltpu.roll(y, quarter, 1) * sin_dn)


def _proj_kernel(*refs, use_rope):
    if use_rope:
        x_ref, mod_ref, gpre_ref, w_ref, qkn_ref, tab_ref, o_ref, ho_ref, h_ref = refs
    else:
        x_ref, mod_ref, gpre_ref, w_ref, qkn_ref, o_ref, ho_ref, h_ref = refs
        tab_ref = None
    j = pl.program_id(1)

    @pl.when(j == 0)
    def _():
        shift = mod_ref[3:4, :]
        scale = mod_ref[4:5, :]
        h = (_rms(x_ref[...], gpre_ref[...]) * (1.0 + scale) + shift).astype(BF16)
        h_ref[...] = h
        ho_ref[...] = h

    y = _dot(h_ref[...], w_ref[...])

    def chunk(c):
        return y[:, c * LANES:(c + 1) * LANES]

    def put(c, v):
        o_ref[:, c * LANES:(c + 1) * LANES] = v.astype(BF16)

    def plain(c):
        put(c, chunk(c))

    def rope_ab(c):
        v = chunk(c)
        put(c, _rope(v, tab_ref, 0, HEAD_DIM // 4) if use_rope else v)

    def rope_c(c):
        v = chunk(c)
        put(c, _rope(v, tab_ref, 3, C_QK_DIM // 4) if use_rope else v)

    def norm_rope(c, row):
        v = _rms(chunk(c), qkn_ref[row:row + 1, :])
        put(c, _rope(v, tab_ref, 0, HEAD_DIM // 4) if use_rope else v)

    def tile_is(lo, hi):
        return jnp.logical_and(j >= lo, j < hi)

    @pl.when(tile_is(0, 2))
    def _():
        for c in range(PROJ_CHUNKS):
            norm_rope(c, 0)

    @pl.when(j == 2)
    def _():
        norm_rope(0, 1)
        norm_rope(1, 1)
        plain(2)
        plain(3)

    @pl.when(tile_is(3, 5))
    def _():
        for c in range(PROJ_CHUNKS):
            rope_ab(c)

    @pl.when(j == 5)
    def _():
        rope_ab(0)
        rope_ab(1)
        plain(2)
        plain(3)

    @pl.when(tile_is(6, 10))
    def _():
        for c in range(PROJ_CHUNKS):
            rope_c(c)

    @pl.when(j >= 10)
    def _():
        for c in range(PROJ_CHUNKS):
            plain(c)


def _project(x, mod, g_pre, w_in, qk_norm, tables, *, tm, rows_per_group, n_pos):
    rows = x.shape[0]
    use_rope = tables is not None
    tiles_per_group = rows_per_group // tm
    in_specs = [
        pl.BlockSpec((tm, D_MODEL), lambda i, j: (i, 0)),
        pl.BlockSpec((None, N_MOD, D_MODEL), lambda i, j: (i // tiles_per_group, 0, 0)),
        pl.BlockSpec((1, D_MODEL), lambda i, j: (0, 0)),
        pl.BlockSpec((D_MODEL, PROJ_TN), lambda i, j: (0, j)),
        pl.BlockSpec((2, HEAD_DIM), lambda i, j: (0, 0)),
    ]
    args = [x, mod, g_pre, w_in, qk_norm]
    if use_rope:
        pos_tiles = n_pos // tm
        in_specs.append(pl.BlockSpec((6, tm, LANES), lambda i, j: (0, i % pos_tiles, 0)))
        args.append(tables)
    return pl.pallas_call(
        functools.partial(_proj_kernel, use_rope=use_rope),
        grid=(rows // tm, QKV_WIDTH // PROJ_TN),
        in_specs=in_specs,
        out_specs=[
            pl.BlockSpec((tm, PROJ_TN), lambda i, j: (i, j)),
            pl.BlockSpec((tm, D_MODEL), lambda i, j: (i, 0)),
        ],
        out_shape=[
            jax.ShapeDtypeStruct((rows, QKV_WIDTH), BF16),
            jax.ShapeDtypeStruct((rows, D_MODEL), BF16),
        ],
        scratch_shapes=[pltpu.VMEM((tm, D_MODEL), BF16)],
        name="in_projection",
        compiler_params=_params(("arbitrary", "arbitrary")),
    )(*args)


def _rope_tables(n_tok):
    rows = n_tok // GRID_W
    row = jnp.repeat(jnp.arange(rows, dtype=jnp.int32), GRID_W).astype(F32)
    col = jnp.tile(jnp.arange(GRID_W, dtype=jnp.int32), rows).astype(F32)
    out = []
    for rot_dim in (HEAD_DIM, C_QK_DIM):
        d_ax = rot_dim // 2
        inv = ROPE_THETA ** (-jnp.arange(0, d_ax, 2, dtype=F32) / d_ax)
        ang_r = row[:, None] * inv[None, :]
        ang_c = col[:, None] * inv[None, :]
        ang = jnp.concatenate([ang_r, ang_r, ang_c, ang_c], axis=-1)
        reps = LANES // rot_dim
        cos = jnp.tile(jnp.cos(ang), (1, reps))
        sin = jnp.tile(jnp.sin(ang), (1, reps))
        quarter = (jnp.arange(LANES) % rot_dim) // (rot_dim // 4)
        even = (quarter % 2 == 0)[None, :]
        out += [cos, jnp.where(even, -sin, 0.0), jnp.where(even, 0.0, sin)]
    return jnp.stack(out, axis=0)


ATT_TK = 512


def _online_step(s, v, m_ref, l_ref, acc_ref):
    m_prev = m_ref[...]
    m_new = jnp.maximum(m_prev, jnp.max(s, axis=-1, keepdims=True))
    alpha = jnp.exp(m_prev - m_new)
    p = jnp.exp(s - m_new)
    l_ref[...] = alpha * l_ref[...] + jnp.sum(p, axis=-1, keepdims=True)
    acc_ref[...] = alpha * acc_ref[...] + _dot(p.astype(BF16), v)
    m_ref[...] = m_new


def _attn_global_kernel(*refs, diff, with_lat, tq, n_lat, scale, lambda_init):
    refs = list(refs)
    q_ref, kc_ref, vc_ref = refs[:3]
    pos = 3
    if with_lat:
        kl_ref, vl_ref = refs[pos:pos + 2]
        pos += 2
    if diff:
        lam_ref, subln_ref = refs[pos:pos + 2]
        pos += 2
    o_ref, qs_ref, m_ref, l_ref, acc_ref = refs[pos:]

    if diff:
        q = q_ref[...]
        lane = lax.broadcasted_iota(jnp.int32, q.shape, 1)
        qs_ref[0:tq, :] = jnp.where(lane < C_QK_DIM, q, jnp.zeros_like(q))
        qs_ref[tq:2 * tq, :] = jnp.where(lane >= C_QK_DIM, q, jnp.zeros_like(q))
        n_stack = 2
    else:
        n_stack = A_HEADS // A_KV_HEADS
        for g in range(n_stack):
            qs_ref[g * tq:(g + 1) * tq, :] = q_ref[:, g * HEAD_DIM:(g + 1) * HEAD_DIM]

    m_ref[...] = jnp.full(m_ref.shape, NEG_INF, F32)
    l_ref[...] = jnp.zeros(l_ref.shape, F32)
    acc_ref[...] = jnp.zeros(acc_ref.shape, F32)

    _online_step(_dot_nt(qs_ref[...], kc_ref[...]) * scale, vc_ref[...], m_ref, l_ref, acc_ref)

    if with_lat:
        def body(c, carry):
            k0 = pl.multiple_of(c * ATT_TK, ATT_TK)
            s = _dot_nt(qs_ref[...], kl_ref[pl.ds(k0, ATT_TK), :]) * scale
            _online_step(s, vl_ref[pl.ds(k0, ATT_TK), :], m_ref, l_ref, acc_ref)
            return carry

        lax.fori_loop(0, n_lat // ATT_TK, body, 0)

    o = acc_ref[...] / l_ref[...]
    if diff:
        lf = lam_ref[...]
        lam = (jnp.exp(jnp.sum(lf[0:1] * lf[1:2], axis=-1, keepdims=True))
               - jnp.exp(jnp.sum(lf[2:3] * lf[3:4], axis=-1, keepdims=True)) + lambda_init)
        d = o[0:tq] - lam * o[tq:2 * tq]
        o_ref[...] = (_rms(d, subln_ref[...]) * (1.0 - lambda_init)).astype(BF16)
    else:
        for g in range(n_stack):
            o_ref[:, g * HEAD_DIM:(g + 1) * HEAD_DIM] = o[g * tq:(g + 1) * tq].astype(BF16)


def _attn_global(q_src, qkv_ctx, qkv_lat, *, diff, tq, n_q, n_ctx, n_lat, batch,
                 lam=None, subln=None, lambda_init=0.0):
    with_lat = qkv_lat is not None
    q_tiles = n_q // tq
    if diff:
        heads, n_stack, q_w, q_unit = C_HEADS, 2, HEAD_DIM, LANES
        q_col, k_col, v_col = C_Q, C_K, C_V
        scale = C_QK_DIM ** -0.5
    else:
        heads, n_stack, q_w = A_KV_HEADS, A_HEADS // A_KV_HEADS, (A_HEADS // A_KV_HEADS) * HEAD_DIM
        q_col, k_col, v_col = A_Q * LANES // q_w, A_K, A_V
        scale = HEAD_DIM ** -0.5
    in_specs = [
        pl.BlockSpec((tq, q_w), lambda b, h, t: (b * q_tiles + t, q_col + h)),
        pl.BlockSpec((n_ctx, HEAD_DIM), lambda b, h, t: (b, k_col + h)),
        pl.BlockSpec((n_ctx, HEAD_DIM), lambda b, h, t: (b, v_col + h)),
    ]
    args = [q_src, qkv_ctx, qkv_ctx]
    if with_lat:
        in_specs += [
            pl.BlockSpec((n_lat, HEAD_DIM), lambda b, h, t: (b, k_col + h)),
            pl.BlockSpec((n_lat, HEAD_DIM), lambda b, h, t: (b, v_col + h)),
        ]
        args += [qkv_lat, qkv_lat]
    if diff:
        in_specs += [
            pl.BlockSpec((4, C_QK_DIM), lambda b, h, t: (0, 0)),
            pl.BlockSpec((1, HEAD_DIM), lambda b, h, t: (0, 0)),
        ]
        args += [lam, subln]
    rows = n_stack * tq
    return pl.pallas_call(
        functools.partial(_attn_global_kernel, diff=diff, with_lat=with_lat, tq=tq, n_lat=n_lat,
                          scale=scale, lambda_init=lambda_init),
        grid=(batch, heads, q_tiles),
        in_specs=in_specs,
        out_specs=pl.BlockSpec((tq, q_w), lambda b, h, t: (b * q_tiles + t, h)),
        out_shape=jax.ShapeDtypeStruct((batch * n_q, BRANCH_WIDTH), BF16),
        scratch_shapes=[
            pltpu.VMEM((rows, HEAD_DIM), BF16),
            pltpu.VMEM((rows, 1), F32),
            pltpu.VMEM((rows, 1), F32),
            pltpu.VMEM((rows, HEAD_DIM), F32),
        ],
        name="attn_diff" if diff else "attn_global",
        compiler_params=_params(("arbitrary", "arbitrary", "arbitrary")),
    )(*args)


WIN_TQ = 256
WIN_SPAN = WIN_TQ + 2 * WINDOW


def _attn_window_kernel(*refs, with_lat, n_lat):
    if with_lat:
        sink_ref, q_ref, kc_ref, vc_ref, kl_ref, vl_ref, o_ref, qs_ref, m_ref, l_ref, acc_ref = refs
    else:
        sink_ref, q_ref, kc_ref, vc_ref, o_ref, qs_ref, m_ref, l_ref, acc_ref = refs
    tq = WIN_TQ
    n_stack = B_HEADS // B_KV_HEADS
    scale = HEAD_DIM ** -0.5
    h = pl.program_id(1)
    t = pl.program_id(2)
    for g in range(n_stack):
        qs_ref[g * tq:(g + 1) * tq, :] = q_ref[:, g * HEAD_DIM:(g + 1) * HEAD_DIM]
        m_ref[g * tq:(g + 1) * tq, :] = jnp.full((tq, 1), sink_ref[h * n_stack + g], F32)
    l_ref[...] = jnp.ones(l_ref.shape, F32)
    acc_ref[...] = jnp.zeros(acc_ref.shape, F32)

    _online_step(_dot_nt(qs_ref[...], kc_ref[...]) * scale, vc_ref[...], m_ref, l_ref, acc_ref)

    if with_lat:
        start = jnp.clip(t * tq - WINDOW, 0, n_lat - WIN_SPAN)
        start = pl.multiple_of(start, WINDOW)
        s = _dot_nt(qs_ref[...], kl_ref[pl.ds(start, WIN_SPAN), :]) * scale
        row = lax.broadcasted_iota(jnp.int32, s.shape, 0)
        col = lax.broadcasted_iota(jnp.int32, s.shape, 1)
        qpos = t * tq + row % tq
        kpos = start + col
        s = jnp.where(jnp.abs(kpos - qpos) <= WINDOW, s, NEG_INF)
        _online_step(s, vl_ref[pl.ds(start, WIN_SPAN), :], m_ref, l_ref, acc_ref)

    o = acc_ref[...] / l_ref[...]
    for g in range(n_stack):
        o_ref[:, g * HEAD_DIM:(g + 1) * HEAD_DIM] = o[g * tq:(g + 1) * tq].astype(BF16)


def _attn_window(q_src, qkv_ctx, qkv_lat, sink, *, n_q, n_ctx, n_lat, batch):
    with_lat = qkv_lat is not None
    tq = WIN_TQ
    q_tiles = n_q // tq
    n_stack = B_HEADS // B_KV_HEADS
    q_w = n_stack * HEAD_DIM
    q_col = B_Q * LANES // q_w
    in_specs = [
        pl.BlockSpec(memory_space=pltpu.SMEM),
        pl.BlockSpec((tq, q_w), lambda b, h, t: (b * q_tiles + t, q_col + h)),
        pl.BlockSpec((n_ctx, HEAD_DIM), lambda b, h, t: (b, B_K + h)),
        pl.BlockSpec((n_ctx, HEAD_DIM), lambda b, h, t: (b, B_V + h)),
    ]
    args = [sink, q_src, qkv_ctx, qkv_ctx]
    if with_lat:
        in_specs += [
            pl.BlockSpec((n_lat, HEAD_DIM), lambda b, h, t: (b, B_K + h)),
            pl.BlockSpec((n_lat, HEAD_DIM), lambda b, h, t: (b, B_V + h)),
        ]
        args += [qkv_lat, qkv_lat]
    rows = n_stack * tq
    return pl.pallas_call(
        functools.partial(_attn_window_kernel, with_lat=with_lat, n_lat=n_lat),
        grid=(batch, B_KV_HEADS, q_tiles),
        in_specs=in_specs,
        out_specs=pl.BlockSpec((tq, q_w), lambda b, h, t: (b * q_tiles + t, h)),
        out_shape=jax.ShapeDtypeStruct((batch * n_q, BRANCH_WIDTH), BF16),
        scratch_shapes=[
            pltpu.VMEM((rows, HEAD_DIM), BF16),
            pltpu.VMEM((rows, 1), F32),
            pltpu.VMEM((rows, 1), F32),
            pltpu.VMEM((rows, HEAD_DIM), F32),
        ],
        name="attn_window",
        compiler_params=_params(("arbitrary", "arbitrary", "arbitrary")),
    )(*args)


MERGE_TN = 1024


def _merge_kernel(h_ref, oa_ref, ob_ref, oc_ref, wg_ref, wb_ref, y_ref, acc_ref):
    r = pl.program_id(2)
    gate = jax.nn.sigmoid(_dot(h_ref[...], wg_ref[...]))
    for k, o_ref in enumerate((oa_ref, ob_ref, oc_ref)):
        @pl.when(r == k)
        def _(o_ref=o_ref, k=k):
            part = gate * _dot(o_ref[...], wb_ref[...])
            if k == 0:
                acc_ref[...] = part
            elif k < N_BRANCH - 1:
                acc_ref[...] += part
            else:
                y_ref[...] = (acc_ref[...] + part).astype(BF16)


def _merge(h, o_a, o_b, o_c, w_in, w_branch, *, tm):
    rows = h.shape[0]
    n_col = D_MODEL // MERGE_TN
    gate_col = QKV_WIDTH // MERGE_TN
    o_spec = pl.BlockSpec((tm, BRANCH_WIDTH), lambda i, n, r: (i, 0))
    return pl.pallas_call(
        _merge_kernel,
        grid=(rows // tm, n_col, N_BRANCH),
        in_specs=[
            pl.BlockSpec((tm, D_MODEL), lambda i, n, r: (i, 0)),
            o_spec, o_spec, o_spec,
            pl.BlockSpec((D_MODEL, MERGE_TN), lambda i, n, r: (0, gate_col + r * n_col + n)),
            pl.BlockSpec((None, BRANCH_WIDTH, MERGE_TN), lambda i, n, r: (r, 0, n)),
        ],
        out_specs=pl.BlockSpec((tm, MERGE_TN), lambda i, n, r: (i, n)),
        out_shape=jax.ShapeDtypeStruct((rows, D_MODEL), BF16),
        scratch_shapes=[pltpu.VMEM((tm, MERGE_TN), F32)],
        name="branch_merge",
        compiler_params=_params(("arbitrary", "arbitrary", "arbitrary")),
    )(h, o_a, o_b, o_c, w_in, w_branch)


def _out_kernel(y_ref, x_ref, mod_ref, gpost_ref, w_ref, o_ref):
    z = _dot(y_ref[...], w_ref[...])
    o_ref[...] = x_ref[...] + mod_ref[5:6, :] * _rms(z, gpost_ref[...])


def _out_project(y, x, mod, g_post, w_out, *, tm, rows_per_group):
    rows = x.shape[0]
    tiles_per_group = rows_per_group // tm
    return pl.pallas_call(
        _out_kernel,
        grid=(rows // tm,),
        in_specs=[
            pl.BlockSpec((tm, D_MODEL), lambda i: (i, 0)),
            pl.BlockSpec((tm, D_MODEL), lambda i: (i, 0)),
            pl.BlockSpec((None, N_MOD, D_MODEL), lambda i: (i // tiles_per_group, 0, 0)),
            pl.BlockSpec((1, D_MODEL), lambda i: (0, 0)),
            pl.BlockSpec((D_MODEL, D_MODEL), lambda i: (0, 0)),
        ],
        out_specs=pl.BlockSpec((tm, D_MODEL), lambda i: (i, 0)),
        out_shape=jax.ShapeDtypeStruct((rows, D_MODEL), F32),
        name="out_projection",
        compiler_params=_params(("arbitrary",)),
    )(y, x, mod, g_post, w_out)


def kernel(x, c, ctx, c_ctx, w_ada, b_ada, norm_pre, norm_post, w_ff_in, w_ff_out, w_in, qk_norm_a,
           sink_b, lam_c, subln_c, w_branch, w_out):
    batch, n_lat, _ = x.shape
    n_ctx = ctx.shape[1]
    depth = w_ada.shape[0]
    assert (depth, x.shape[2]) == (DEPTH, D_MODEL)

    tm_lat_ffn, tm_lat_proj, tm_lat_merge, tm_lat_out = 512, 1024, 1024, 512
    tm_ctx = batch * n_ctx

    cvec = jnp.concatenate([c, c_ctx[None, :]], axis=0)
    mod_all = _modulation(cvec, w_ada, b_ada)
    tables = _rope_tables(n_lat)

    w_ff_in_h = w_ff_in.astype(BF16)
    w_ff_out_h = w_ff_out.astype(BF16)
    w_in_h = w_in.astype(BF16)
    w_branch_h = w_branch.astype(BF16)
    w_out_h = w_out.astype(BF16)

    x_lat = x.reshape(batch * n_lat, D_MODEL)
    x_ctx = ctx.reshape(batch * n_ctx, D_MODEL)

    for l in range(depth):
        last = l == depth - 1
        lambda_init = 0.8 - 0.6 * math.exp(-0.3 * l)
        mod_lat = mod_all[l, 0:batch].reshape(batch, N_MOD, D_MODEL)
        mod_ctx = mod_all[l, batch:batch + 1].reshape(1, N_MOD, D_MODEL)
        g_pre = [norm_pre[l, s][None, :] for s in range(3)]
        g_post = [norm_post[l, s][None, :] for s in range(3)]
        lat = dict(rows_per_group=n_lat)
        cx = dict(rows_per_group=tm_ctx)

        x_ctx = _ffn_half_step(x_ctx, mod_ctx, g_pre[0], g_post[0], w_ff_in_h[l, 0], w_ff_out_h[l, 0],
                               mod_row=0, tm=tm_ctx, **cx)
        x_lat = _ffn_half_step(x_lat, mod_lat, g_pre[0], g_post[0], w_ff_in_h[l, 0], w_ff_out_h[l, 0],
                               mod_row=0, tm=tm_lat_ffn, **lat)

        qkv_ctx, h_ctx = _project(x_ctx, mod_ctx, g_pre[1], w_in_h[l], qk_norm_a[l], None,
                                  tm=tm_ctx, n_pos=n_lat, **cx)
        qkv_lat, h_lat = _project(x_lat, mod_lat, g_pre[1], w_in_h[l], qk_norm_a[l], tables,
                                  tm=tm_lat_proj, n_pos=n_lat, **lat)

        dims = dict(n_ctx=n_ctx, n_lat=n_lat, batch=batch)
        diff_args = dict(lam=lam_c[l], subln=subln_c[l][None, :], lambda_init=lambda_init)
        a_lat = _attn_global(qkv_lat, qkv_ctx, qkv_lat, diff=False, tq=128, n_q=n_lat, **dims)
        b_lat = _attn_window(qkv_lat, qkv_ctx, qkv_lat, sink_b[l], n_q=n_lat, **dims)
        c_lat = _attn_global(qkv_lat, qkv_ctx, qkv_lat, diff=True, tq=256, n_q=n_lat, **dims, **diff_args)
        y_lat = _merge(h_lat, a_lat, b_lat, c_lat, w_in_h[l], w_branch_h[l], tm=tm_lat_merge)
        x_lat = _out_project(y_lat, x_lat, mod_lat, g_post[1], w_out_h[l], tm=tm_lat_out, **lat)
        x_lat = _ffn_half_step(x_lat, mod_lat, g_pre[2], g_post[2], w_ff_in_h[l, 1], w_ff_out_h[l, 1],
                               mod_row=6, tm=tm_lat_ffn, **lat)

        if not last:
            a_ctx = _attn_global(qkv_ctx, qkv_ctx, None, diff=False, tq=n_ctx, n_q=n_ctx, **dims)
            b_ctx = _attn_window(qkv_ctx, qkv_ctx, None, sink_b[l], n_q=n_ctx, **dims)
            c_ctx_o = _attn_global(qkv_ctx, qkv_ctx, None, diff=True, tq=n_ctx, n_q=n_ctx, **dims, **diff_args)
            y_ctx = _merge(h_ctx, a_ctx, b_ctx, c_ctx_o, w_in_h[l], w_branch_h[l], tm=tm_ctx)
            x_ctx = _out_project(y_ctx, x_ctx, mod_ctx, g_post[1], w_out_h[l], tm=tm_ctx, **cx)
            x_ctx = _ffn_half_step(x_ctx, mod_ctx, g_pre[2], g_post[2], w_ff_in_h[l, 1], w_ff_out_h[l, 1],
                                   mod_row=6, tm=tm_ctx, **cx)

    return x_lat.reshape(batch, n_lat, D_MODEL)
```

```python
import functools
import math

import jax
import jax.numpy as jnp
from jax import lax
from jax.experimental import pallas as pl
from jax.experimental.pallas import tpu as pltpu

D_MODEL = 2048
DEPTH = 4
GRID_W = 64
HEAD_DIM = 128
A_HEADS = 8
A_KV_HEADS = 2
B_HEADS = 8
B_KV_HEADS = 2
C_HEADS = 8
C_QK_DIM = HEAD_DIM // 2
BRANCH_WIDTH = 1024
N_BRANCH = 3
WINDOW = 128
D_FF = 5632
HALF_STEP = 0.5
ROPE_THETA = 10000.0
NORM_EPS = 1e-6
NEG_INF = -1e30
N_MOD = 9
QKV_WIDTH = 6144
LANES = 128
SUBLANES = 8
VMEM_LIMIT = 56 * 1024 * 1024

A_Q, A_K, A_V = 0, 8, 10
B_Q, B_K, B_V = 12, 20, 22
C_Q, C_K, C_V = 24, 32, 40

F32 = jnp.float32
BF16 = jnp.bfloat16

LOG2_E = math.log2(math.e)
Q_GAIN_AB = HEAD_DIM ** -0.5 * LOG2_E
Q_GAIN_C = C_QK_DIM ** -0.5 * LOG2_E


def _params(sem, vmem=VMEM_LIMIT):
    return pltpu.CompilerParams(dimension_semantics=sem, vmem_limit_bytes=vmem)


def _rms(x, g):
    ms = jnp.mean(x * x, axis=-1, keepdims=True)
    return x * lax.rsqrt(ms + NORM_EPS) * g


def _dot(a, b):
    return jnp.dot(a, b, preferred_element_type=F32)


def _dot_nt(a, b):
    return lax.dot_general(a, b, (((1,), (1,)), ((), ())), preferred_element_type=F32)


MOD_TN = 1024


def _mod_kernel(c_ref, w_ref, b_ref, o_ref):
    n_rep = MOD_TN // LANES

    def body(kc, accs):
        r0 = pl.multiple_of(kc * SUBLANES, SUBLANES)
        wk = w_ref[0, pl.ds(r0, SUBLANES), :]
        out = []
        for m in range(3):
            cm = c_ref[m, pl.ds(r0, SUBLANES), :]
            sm = cm * jax.nn.sigmoid(cm)
            out.append(accs[m] + jnp.concatenate([sm] * n_rep, axis=1) * wk)
        return tuple(out)

    zero = jnp.zeros((SUBLANES, MOD_TN), F32)
    accs = lax.fori_loop(0, D_MODEL // SUBLANES, body, (zero, zero, zero), unroll=4)
    bias = b_ref[0]
    for m in range(3):
        o_ref[0, m:m + 1, :] = jnp.sum(accs[m], axis=0, keepdims=True) + bias
    o_ref[0, 3:SUBLANES, :] = jnp.zeros((SUBLANES - 3, MOD_TN), F32)


def _modulation(cvec, w_ada, b_ada):
    width = N_MOD * D_MODEL
    c_b = jnp.broadcast_to(cvec[:, :, None], (3, D_MODEL, LANES))
    return pl.pallas_call(
        _mod_kernel,
        grid=(DEPTH, width // MOD_TN),
        in_specs=[
            pl.BlockSpec((3, D_MODEL, LANES), lambda l, j: (0, 0, 0)),
            pl.BlockSpec((1, D_MODEL, MOD_TN), lambda l, j: (l, 0, j)),
            pl.BlockSpec((1, 1, MOD_TN), lambda l, j: (l, 0, j)),
        ],
        out_specs=pl.BlockSpec((1, SUBLANES, MOD_TN), lambda l, j: (l, 0, j)),
        out_shape=jax.ShapeDtypeStruct((DEPTH, SUBLANES, width), F32),
        name="adaln_modulation",
        compiler_params=_params(("arbitrary", "arbitrary")),
    )(c_b, w_ada, b_ada.reshape(DEPTH, 1, width))


FFN_TF = 512


def _ffn_kernel(x_ref, mod_ref, gpre_ref, gpost_ref, w1g_ref, w1u_ref, w2_ref, o_ref,
                h_ref, acc_ref, *, mod_row):
    j = pl.program_id(1)

    @pl.when(j == 0)
    def _():
        shift = mod_ref[mod_row:mod_row + 1, :]
        scale = mod_ref[mod_row + 1:mod_row + 2, :]
        h = _rms(x_ref[...], gpre_ref[...]) * (1.0 + scale) + shift
        h_ref[...] = h.astype(BF16)

    h = h_ref[...]
    g = _dot(h, w1g_ref[...])
    u = _dot(h, w1u_ref[...])
    a = (g * jax.nn.sigmoid(g)) * u
    part = _dot(a.astype(BF16), w2_ref[...])

    @pl.when(j == 0)
    def _():
        acc_ref[...] = part

    @pl.when(j > 0)
    def _():
        acc_ref[...] += part

    @pl.when(j == pl.num_programs(1) - 1)
    def _():
        gate = mod_ref[mod_row + 2:mod_row + 3, :]
        y = _rms(acc_ref[...], gpost_ref[...])
        o_ref[...] = x_ref[...] + HALF_STEP * gate * y


def _ffn_half_step(x, mod, g_pre, g_post, w1, w2, *, mod_row, tm, rows_per_group):
    rows = x.shape[0]
    nf = D_FF // FFN_TF
    tiles_per_group = rows_per_group // tm
    return pl.pallas_call(
        functools.partial(_ffn_kernel, mod_row=mod_row),
        grid=(rows // tm, nf),
        in_specs=[
            pl.BlockSpec((tm, D_MODEL), lambda i, j: (i, 0)),
            pl.BlockSpec((None, N_MOD, D_MODEL), lambda i, j: (i // tiles_per_group, 0, 0)),
            pl.BlockSpec((1, D_MODEL), lambda i, j: (0, 0)),
            pl.BlockSpec((1, D_MODEL), lambda i, j: (0, 0)),
            pl.BlockSpec((D_MODEL, FFN_TF), lambda i, j: (0, j)),
            pl.BlockSpec((D_MODEL, FFN_TF), lambda i, j: (0, nf + j)),
            pl.BlockSpec((FFN_TF, D_MODEL), lambda i, j: (j, 0)),
        ],
        out_specs=pl.BlockSpec((tm, D_MODEL), lambda i, j: (i, 0)),
        out_shape=jax.ShapeDtypeStruct((rows, D_MODEL), F32),
        scratch_shapes=[pltpu.VMEM((tm, D_MODEL), BF16), pltpu.VMEM((tm, D_MODEL), F32)],
        name="ffn_half_step",
        compiler_params=_params(("arbitrary", "arbitrary")),
    )(x, mod, g_pre, g_post, w1, w1, w2)


PROJ_TN = 512
PROJ_CHUNKS = PROJ_TN // LANES


def _rope(y, tab_ref, base, quarter):
    cos, sin_up, sin_dn = tab_ref[base], tab_ref[base + 1], tab_ref[base + 2]
    return (y * cos + pltpu.roll(y, LANES - quarter, 1) * sin_up + pltpu.roll(y, quarter, 1) * sin_dn)


def _proj_kernel(*refs, use_rope):
    if use_rope:
        x_ref, mod_ref, gpre_ref, w_ref, qkn_ref, tab_ref, o_ref, ho_ref, h_ref = refs
    else:
        x_ref, mod_ref, gpre_ref, w_ref, qkn_ref, o_ref, ho_ref, h_ref = refs
        tab_ref = None
    j = pl.program_id(1)

    @pl.when(j == 0)
    def _():
        shift = mod_ref[3:4, :]
        scale = mod_ref[4:5, :]
        h = (_rms(x_ref[...], gpre_ref[...]) * (1.0 + scale) + shift).astype(BF16)
        h_ref[...] = h
        ho_ref[...] = h

    y = _dot(h_ref[...], w_ref[...])

    def chunk(c):
        return y[:, c * LANES:(c + 1) * LANES]

    def put(c, v):
        o_ref[:, c * LANES:(c + 1) * LANES] = v.astype(BF16)

    def plain(c):
        put(c, chunk(c))

    def rope_ab(c, gain=None):
        v = chunk(c)
        v = _rope(v, tab_ref, 0, HEAD_DIM // 4) if use_rope else v
        put(c, v if gain is None else v * gain)

    def rope_c(c, gain=None):
        v = chunk(c)
        v = _rope(v, tab_ref, 3, C_QK_DIM // 4) if use_rope else v
        put(c, v if gain is None else v * gain)

    def norm_rope(c, row, gain=None):
        v = _rms(chunk(c), qkn_ref[row:row + 1, :])
        v = _rope(v, tab_ref, 0, HEAD_DIM // 4) if use_rope else v
        put(c, v if gain is None else v * gain)

    def tile_is(lo, hi):
        return jnp.logical_and(j >= lo, j < hi)

    @pl.when(tile_is(0, 2))
    def _():
        for c in range(PROJ_CHUNKS):
            norm_rope(c, 0, Q_GAIN_AB)

    @pl.when(j == 2)
    def _():
        norm_rope(0, 1)
        norm_rope(1, 1)
        plain(2)
        plain(3)

    @pl.when(tile_is(3, 5))
    def _():
        for c in range(PROJ_CHUNKS):
            rope_ab(c, Q_GAIN_AB)

    @pl.when(j == 5)
    def _():
        rope_ab(0)
        rope_ab(1)
        plain(2)
        plain(3)

    @pl.when(tile_is(6, 8))
    def _():
        for c in range(PROJ_CHUNKS):
            rope_c(c, Q_GAIN_C)

    @pl.when(tile_is(8, 10))
    def _():
        for c in range(PROJ_CHUNKS):
            rope_c(c)

    @pl.when(j >= 10)
    def _():
        for c in range(PROJ_CHUNKS):
            plain(c)


def _project(x, mod, g_pre, w_in, qk_norm, tables, *, tm, rows_per_group, n_pos):
    rows = x.shape[0]
    use_rope = tables is not None
    tiles_per_group = rows_per_group // tm
    in_specs = [
        pl.BlockSpec((tm, D_MODEL), lambda i, j: (i, 0)),
        pl.BlockSpec((None, N_MOD, D_MODEL), lambda i, j: (i // tiles_per_group, 0, 0)),
        pl.BlockSpec((1, D_MODEL), lambda i, j: (0, 0)),
        pl.BlockSpec((D_MODEL, PROJ_TN), lambda i, j: (0, j)),
        pl.BlockSpec((2, HEAD_DIM), lambda i, j: (0, 0)),
    ]
    args = [x, mod, g_pre, w_in, qk_norm]
    if use_rope:
        pos_tiles = n_pos // tm
        in_specs.append(pl.BlockSpec((6, tm, LANES), lambda i, j: (0, i % pos_tiles, 0)))
        args.append(tables)
    return pl.pallas_call(
        functools.partial(_proj_kernel, use_rope=use_rope),
        grid=(rows // tm, QKV_WIDTH // PROJ_TN),
        in_specs=in_specs,
        out_specs=[
            pl.BlockSpec((tm, PROJ_TN), lambda i, j: (i, j)),
            pl.BlockSpec((tm, D_MODEL), lambda i, j: (i, 0)),
        ],
        out_shape=[
            jax.ShapeDtypeStruct((rows, QKV_WIDTH), BF16),
            jax.ShapeDtypeStruct((rows, D_MODEL), BF16),
        ],
        scratch_shapes=[pltpu.VMEM((tm, D_MODEL), BF16)],
        name="in_projection",
        compiler_params=_params(("arbitrary", "arbitrary")),
    )(*args)


def _rope_tables(n_tok):
    rows = n_tok // GRID_W
    row = jnp.repeat(jnp.arange(rows, dtype=jnp.int32), GRID_W).astype(F32)
    col = jnp.tile(jnp.arange(GRID_W, dtype=jnp.int32), rows).astype(F32)
    out = []
    for rot_dim in (HEAD_DIM, C_QK_DIM):
        d_ax = rot_dim // 2
        inv = ROPE_THETA ** (-jnp.arange(0, d_ax, 2, dtype=F32) / d_ax)
        ang_r = row[:, None] * inv[None, :]
        ang_c = col[:, None] * inv[None, :]
        ang = jnp.concatenate([ang_r, ang_r, ang_c, ang_c], axis=-1)
        reps = LANES // rot_dim
        cos = jnp.tile(jnp.cos(ang), (1, reps))
        sin = jnp.tile(jnp.sin(ang), (1, reps))
        quarter = (jnp.arange(LANES) % rot_dim) // (rot_dim // 4)
        even = (quarter % 2 == 0)[None, :]
        out += [cos, jnp.where(even, -sin, 0.0), jnp.where(even, 0.0, sin)]
    return jnp.stack(out, axis=0)


ATT_TK = 512
ATT_RB = 128


def _softmax_block(s, v, rows, m_ref, l_ref, acc_ref):
    blocks = [s[:, j * LANES:(j + 1) * LANES] for j in range(s.shape[1] // LANES)]
    m_prev = m_ref[rows, :]
    m_new = jnp.maximum(m_prev, jnp.max(functools.reduce(jnp.maximum, blocks), axis=-1, keepdims=True))
    alpha = jnp.exp2(m_prev - m_new)
    p = [jnp.exp2(b - m_new) for b in blocks]
    l_ref[rows, :] = alpha * l_ref[rows, :] + functools.reduce(jnp.add, p)
    pv = _dot(jnp.concatenate(p, axis=1).astype(BF16), v)
    acc_ref[rows, :] = alpha * acc_ref[rows, :] + pv
    m_ref[rows, :] = m_new


def _attend_chunk(qs_ref, k, v, m_ref, l_ref, acc_ref, mask_fn=None):
    for r in range(qs_ref.shape[0] // ATT_RB):
        rows = slice(r * ATT_RB, (r + 1) * ATT_RB)
        s = _dot_nt(qs_ref[rows, :], k)
        if mask_fn is not None:
            s = mask_fn(s, r * ATT_RB)
        _softmax_block(s, v, rows, m_ref, l_ref, acc_ref)


def _pipelined_keys(qs_ref, k_ref, v_ref, m_ref, l_ref, acc_ref, pipe_refs, n_chunks):
    s_refs, p_refs, a_refs = pipe_refs[0:2], pipe_refs[2:4], pipe_refs[4:6]
    n_rows = qs_ref.shape[0]
    assert n_chunks % 2 == 0

    def chunk_start(c):
        return pl.multiple_of(jnp.clip(c, 0, n_chunks - 1) * ATT_TK, ATT_TK)

    def scores(c, buf):
        s_refs[buf][...] = _dot_nt(qs_ref[...], k_ref[pl.ds(chunk_start(c), ATT_TK), :])

    def softmax(buf):
        for r in range(n_rows // ATT_RB):
            rows = slice(r * ATT_RB, (r + 1) * ATT_RB)
            s = s_refs[buf][rows, :]
            blocks = [s[:, j * LANES:(j + 1) * LANES] for j in range(ATT_TK // LANES)]
            m_prev = m_ref[rows, :]
            m_new = jnp.maximum(
                m_prev, jnp.max(functools.reduce(jnp.maximum, blocks), axis=-1, keepdims=True))
            alpha = jnp.exp2(m_prev - m_new)
            p = [jnp.exp2(b - m_new) for b in blocks]
            l_ref[rows, :] = alpha * l_ref[rows, :] + functools.reduce(jnp.add, p)
            m_ref[rows, :] = m_new
            a_refs[buf][rows, :] = alpha
            p_refs[buf][rows, :] = jnp.concatenate(p, axis=1).astype(BF16)

    def values(c, buf):
        pv = _dot(p_refs[buf][...], v_ref[pl.ds(chunk_start(c), ATT_TK), :])
        acc_ref[...] = a_refs[buf][...] * acc_ref[...] + pv

    p_refs[1][...] = jnp.zeros(p_refs[1].shape, BF16)
    a_refs[1][...] = jnp.ones(a_refs[1].shape, F32)
    scores(0, 0)

    def body(i, carry):
        c = 2 * i
        values(c - 1, 1)
        softmax(0)
        scores(c + 1, 1)
        values(c, 0)
        softmax(1)
        scores(c + 2, 0)
        return carry

    lax.fori_loop(0, n_chunks // 2, body, 0)
    values(n_chunks - 1, 1)


def _attend_result(l_ref, acc_ref):
    return acc_ref[...] / jnp.sum(l_ref[...], axis=-1, keepdims=True)


def _attn_global_kernel(*refs, diff, with_lat, tq, n_lat, lambda_init):
    refs = list(refs)
    q_ref, kc_ref, vc_ref = refs[:3]
    pos = 3
    if with_lat:
        kl_ref, vl_ref = refs[pos:pos + 2]
        pos += 2
    if diff:
        lam_ref, subln_ref = refs[pos:pos + 2]
        pos += 2
    o_ref, qs_ref, m_ref, l_ref, acc_ref = refs[pos:pos + 5]
    pipe_refs = refs[pos + 5:]

    if diff:
        q = q_ref[...]
        lane = lax.broadcasted_iota(jnp.int32, q.shape, 1)
        qs_ref[0:tq, :] = jnp.where(lane < C_QK_DIM, q, jnp.zeros_like(q))
        qs_ref[tq:2 * tq, :] = jnp.where(lane >= C_QK_DIM, q, jnp.zeros_like(q))
        n_stack = 2
    else:
        n_stack = A_HEADS // A_KV_HEADS
        for g in range(n_stack):
            qs_ref[g * tq:(g + 1) * tq, :] = q_ref[:, g * HEAD_DIM:(g + 1) * HEAD_DIM]

    m_ref[...] = jnp.full(m_ref.shape, NEG_INF, F32)
    l_ref[...] = jnp.zeros(l_ref.shape, F32)
    acc_ref[...] = jnp.zeros(acc_ref.shape, F32)

    _attend_chunk(qs_ref, kc_ref[...], vc_ref[...], m_ref, l_ref, acc_ref)

    if with_lat:
        _pipelined_keys(qs_ref, kl_ref, vl_ref, m_ref, l_ref, acc_ref, pipe_refs, n_lat // ATT_TK)

    o = _attend_result(l_ref, acc_ref)
    if diff:
        lf = lam_ref[...]
        lam = (jnp.exp(jnp.sum(lf[0:1] * lf[1:2], axis=-1, keepdims=True))
               - jnp.exp(jnp.sum(lf[2:3] * lf[3:4], axis=-1, keepdims=True)) + lambda_init)
        d = o[0:tq] - lam * o[tq:2 * tq]
        o_ref[...] = (_rms(d, subln_ref[...]) * (1.0 - lambda_init)).astype(BF16)
    else:
        for g in range(n_stack):
            o_ref[:, g * HEAD_DIM:(g + 1) * HEAD_DIM] = o[g * tq:(g + 1) * tq].astype(BF16)


def _attn_global(q_src, qkv_ctx, qkv_lat, *, diff, tq, n_q, n_ctx, n_lat, batch,
                 lam=None, subln=None, lambda_init=0.0):
    with_lat = qkv_lat is not None
    q_tiles = n_q // tq
    if diff:
        heads, n_stack, q_w = C_HEADS, 2, HEAD_DIM
        q_col, k_col, v_col = C_Q, C_K, C_V
    else:
        heads, n_stack, q_w = A_KV_HEADS, A_HEADS // A_KV_HEADS, (A_HEADS // A_KV_HEADS) * HEAD_DIM
        q_col, k_col, v_col = A_Q * LANES // q_w, A_K, A_V
    in_specs = [
        pl.BlockSpec((tq, q_w), lambda b, h, t: (b * q_tiles + t, q_col + h)),
        pl.BlockSpec((n_ctx, HEAD_DIM), lambda b, h, t: (b, k_col + h)),
        pl.BlockSpec((n_ctx, HEAD_DIM), lambda b, h, t: (b, v_col + h)),
    ]
    args = [q_src, qkv_ctx, qkv_ctx]
    if with_lat:
        in_specs += [
            pl.BlockSpec((n_lat, HEAD_DIM), lambda b, h, t: (b, k_col + h)),
            pl.BlockSpec((n_lat, HEAD_DIM), lambda b, h, t: (b, v_col + h)),
        ]
        args += [qkv_lat, qkv_lat]
    if diff:
        in_specs += [
            pl.BlockSpec((4, C_QK_DIM), lambda b, h, t: (0, 0)),
            pl.BlockSpec((1, HEAD_DIM), lambda b, h, t: (0, 0)),
        ]
        args += [lam, subln]
    rows = n_stack * tq
    scratch = [
        pltpu.VMEM((rows, HEAD_DIM), BF16),
        pltpu.VMEM((rows, LANES), F32),
        pltpu.VMEM((rows, LANES), F32),
        pltpu.VMEM((rows, HEAD_DIM), F32),
    ]
    if with_lat:
        scratch += ([pltpu.VMEM((rows, ATT_TK), F32)] * 2 + [pltpu.VMEM((rows, ATT_TK), BF16)] * 2
                    + [pltpu.VMEM((rows, LANES), F32)] * 2)
    return pl.pallas_call(
        functools.partial(_attn_global_kernel, diff=diff, with_lat=with_lat, tq=tq, n_lat=n_lat,
                          lambda_init=lambda_init),
        grid=(batch, heads, q_tiles),
        in_specs=in_specs,
        out_specs=pl.BlockSpec((tq, q_w), lambda b, h, t: (b * q_tiles + t, h)),
        out_shape=jax.ShapeDtypeStruct((batch * n_q, BRANCH_WIDTH), BF16),
        scratch_shapes=scratch,
        name="attn_diff" if diff else "attn_global",
        compiler_params=_params(("arbitrary", "arbitrary", "arbitrary")),
    )(*args)


WIN_TQ = 256
WIN_SPAN = WIN_TQ + 2 * WINDOW


def _attn_window_kernel(*refs, with_lat, n_lat):
    if with_lat:
        sink_ref, q_ref, kc_ref, vc_ref, kl_ref, vl_ref, o_ref, qs_ref, m_ref, l_ref, acc_ref = refs
    else:
        sink_ref, q_ref, kc_ref, vc_ref, o_ref, qs_ref, m_ref, l_ref, acc_ref = refs
    tq = WIN_TQ
    n_stack = B_HEADS // B_KV_HEADS
    h = pl.program_id(1)
    t = pl.program_id(2)
    for g in range(n_stack):
        qs_ref[g * tq:(g + 1) * tq, :] = q_ref[:, g * HEAD_DIM:(g + 1) * HEAD_DIM]
        m_ref[g * tq:(g + 1) * tq, :] = jnp.full((tq, LANES), sink_ref[h * n_stack + g] * LOG2_E, F32)
    l_ref[...] = jnp.full(l_ref.shape, 1.0 / LANES, F32)
    acc_ref[...] = jnp.zeros(acc_ref.shape, F32)

    _attend_chunk(qs_ref, kc_ref[...], vc_ref[...], m_ref, l_ref, acc_ref)

    if with_lat:
        start = jnp.clip(t * tq - WINDOW, 0, n_lat - WIN_SPAN)
        start = pl.multiple_of(start, WINDOW)

        def band(s, row0):
            row = lax.broadcasted_iota(jnp.int32, s.shape, 0)
            col = lax.broadcasted_iota(jnp.int32, s.shape, 1)
            qpos = t * tq + row0 % tq + row
            kpos = start + col
            return jnp.where(jnp.abs(kpos - qpos) <= WINDOW, s, NEG_INF)

        _attend_chunk(qs_ref, kl_ref[pl.ds(start, WIN_SPAN), :], vl_ref[pl.ds(start, WIN_SPAN), :],
                      m_ref, l_ref, acc_ref, mask_fn=band)

    o = _attend_result(l_ref, acc_ref)
    for g in range(n_stack):
        o_ref[:, g * HEAD_DIM:(g + 1) * HEAD_DIM] = o[g * tq:(g + 1) * tq].astype(BF16)


def _attn_window(q_src, qkv_ctx, qkv_lat, sink, *, n_q, n_ctx, n_lat, batch):
    with_lat = qkv_lat is not None
    tq = WIN_TQ
    q_tiles = n_q // tq
    n_stack = B_HEADS // B_KV_HEADS
    q_w = n_stack * HEAD_DIM
    q_col = B_Q * LANES // q_w
    in_specs = [
        pl.BlockSpec(memory_space=pltpu.SMEM),
        pl.BlockSpec((tq, q_w), lambda b, h, t: (b * q_tiles + t, q_col + h)),
        pl.BlockSpec((n_ctx, HEAD_DIM), lambda b, h, t: (b, B_K + h)),
        pl.BlockSpec((n_ctx, HEAD_DIM), lambda b, h, t: (b, B_V + h)),
    ]
    args = [sink, q_src, qkv_ctx, qkv_ctx]
    if with_lat:
        in_specs += [
            pl.BlockSpec((n_lat, HEAD_DIM), lambda b, h, t: (b, B_K + h)),
            pl.BlockSpec((n_lat, HEAD_DIM), lambda b, h, t: (b, B_V + h)),
        ]
        args += [qkv_lat, qkv_lat]
    rows = n_stack * tq
    return pl.pallas_call(
        functools.partial(_attn_window_kernel, with_lat=with_lat, n_lat=n_lat),
        grid=(batch, B_KV_HEADS, q_tiles),
        in_specs=in_specs,
        out_specs=pl.BlockSpec((tq, q_w), lambda b, h, t: (b * q_tiles + t, h)),
        out_shape=jax.ShapeDtypeStruct((batch * n_q, BRANCH_WIDTH), BF16),
        scratch_shapes=[
            pltpu.VMEM((rows, HEAD_DIM), BF16),
            pltpu.VMEM((rows, LANES), F32),
            pltpu.VMEM((rows, LANES), F32),
            pltpu.VMEM((rows, HEAD_DIM), F32),
        ],
        name="attn_window",
        compiler_params=_params(("arbitrary", "arbitrary", "arbitrary")),
    )(*args)


MERGE_TN = 1024


def _merge_kernel(h_ref, oa_ref, ob_ref, oc_ref, wg_ref, wb_ref, y_ref, acc_ref):
    r = pl.program_id(2)
    gate = jax.nn.sigmoid(_dot(h_ref[...], wg_ref[...]))
    for k, o_ref in enumerate((oa_ref, ob_ref, oc_ref)):
        @pl.when(r == k)
        def _(o_ref=o_ref, k=k):
            part = gate * _dot(o_ref[...], wb_ref[...])
            if k == 0:
                acc_ref[...] = part
            elif k < N_BRANCH - 1:
                acc_ref[...] += part
            else:
                y_ref[...] = (acc_ref[...] + part).astype(BF16)


def _merge(h, o_a, o_b, o_c, w_in, w_branch, *, tm):
    rows = h.shape[0]
    n_col = D_MODEL // MERGE_TN
    gate_col = QKV_WIDTH // MERGE_TN
    o_spec = pl.BlockSpec((tm, BRANCH_WIDTH), lambda i, n, r: (i, 0))
    return pl.pallas_call(
        _merge_kernel,
        grid=(rows // tm, n_col, N_BRANCH),
        in_specs=[
            pl.BlockSpec((tm, D_MODEL), lambda i, n, r: (i, 0)),
            o_spec, o_spec, o_spec,
            pl.BlockSpec((D_MODEL, MERGE_TN), lambda i, n, r: (0, gate_col + r * n_col + n)),
            pl.BlockSpec((None, BRANCH_WIDTH, MERGE_TN), lambda i, n, r: (r, 0, n)),
        ],
        out_specs=pl.BlockSpec((tm, MERGE_TN), lambda i, n, r: (i, n)),
        out_shape=jax.ShapeDtypeStruct((rows, D_MODEL), BF16),
        scratch_shapes=[pltpu.VMEM((tm, MERGE_TN), F32)],
        name="branch_merge",
        compiler_params=_params(("arbitrary", "arbitrary", "arbitrary")),
    )(h, o_a, o_b, o_c, w_in, w_branch)


def _out_kernel(y_ref, x_ref, mod_ref, gpost_ref, w_ref, o_ref):
    z = _dot(y_ref[...], w_ref[...])
    o_ref[...] = x_ref[...] + mod_ref[5:6, :] * _rms(z, gpost_ref[...])


def _out_project(y, x, mod, g_post, w_out, *, tm, rows_per_group):
    rows = x.shape[0]
    tiles_per_group = rows_per_group // tm
    return pl.pallas_call(
        _out_kernel,
        grid=(rows // tm,),
        in_specs=[
            pl.BlockSpec((tm, D_MODEL), lambda i: (i, 0)),
            pl.BlockSpec((tm, D_MODEL), lambda i: (i, 0)),
            pl.BlockSpec((None, N_MOD, D_MODEL), lambda i: (i // tiles_per_group, 0, 0)),
            pl.BlockSpec((1, D_MODEL), lambda i: (0, 0)),
            pl.BlockSpec((D_MODEL, D_MODEL), lambda i: (0, 0)),
        ],
        out_specs=pl.BlockSpec((tm, D_MODEL), lambda i: (i, 0)),
        out_shape=jax.ShapeDtypeStruct((rows, D_MODEL), F32),
        name="out_projection",
        compiler_params=_params(("arbitrary",)),
    )(y, x, mod, g_post, w_out)


def kernel(x, c, ctx, c_ctx, w_ada, b_ada, norm_pre, norm_post, w_ff_in, w_ff_out, w_in, qk_norm_a,
           sink_b, lam_c, subln_c, w_branch, w_out):
    batch, n_lat, _ = x.shape
    n_ctx = ctx.shape[1]
    depth = w_ada.shape[0]
    assert (depth, x.shape[2]) == (DEPTH, D_MODEL)

    tm_lat_ffn, tm_lat_proj, tm_lat_merge, tm_lat_out = 512, 1024, 1024, 512
    tm_ctx = batch * n_ctx

    cvec = jnp.concatenate([c, c_ctx[None, :]], axis=0)
    mod_all = _modulation(cvec, w_ada, b_ada)
    tables = _rope_tables(n_lat)

    w_ff_in_h = w_ff_in.astype(BF16)
    w_ff_out_h = w_ff_out.astype(BF16)
    w_in_h = w_in.astype(BF16)
    w_branch_h = w_branch.astype(BF16)
    w_out_h = w_out.astype(BF16)

    x_lat = x.reshape(batch * n_lat, D_MODEL)
    x_ctx = ctx.reshape(batch * n_ctx, D_MODEL)

    for l in range(depth):
        last = l == depth - 1
        lambda_init = 0.8 - 0.6 * math.exp(-0.3 * l)
        mod_lat = mod_all[l, 0:batch].reshape(batch, N_MOD, D_MODEL)
        mod_ctx = mod_all[l, batch:batch + 1].reshape(1, N_MOD, D_MODEL)
        g_pre = [norm_pre[l, s][None, :] for s in range(3)]
        g_post = [norm_post[l, s][None, :] for s in range(3)]
        lat = dict(rows_per_group=n_lat)
        cx = dict(rows_per_group=tm_ctx)

        x_ctx = _ffn_half_step(x_ctx, mod_ctx, g_pre[0], g_post[0], w_ff_in_h[l, 0], w_ff_out_h[l, 0],
                               mod_row=0, tm=tm_ctx, **cx)
        x_lat = _ffn_half_step(x_lat, mod_lat, g_pre[0], g_post[0], w_ff_in_h[l, 0], w_ff_out_h[l, 0],
                               mod_row=0, tm=tm_lat_ffn, **lat)

        qkv_ctx, h_ctx = _project(x_ctx, mod_ctx, g_pre[1], w_in_h[l], qk_norm_a[l], None,
                                  tm=tm_ctx, n_pos=n_lat, **cx)
        qkv_lat, h_lat = _project(x_lat, mod_lat, g_pre[1], w_in_h[l], qk_norm_a[l], tables,
                                  tm=tm_lat_proj, n_pos=n_lat, **lat)

        dims = dict(n_ctx=n_ctx, n_lat=n_lat, batch=batch)
        diff_args = dict(lam=lam_c[l], subln=subln_c[l][None, :], lambda_init=lambda_init)
        a_lat = _attn_global(qkv_lat, qkv_ctx, qkv_lat, diff=False, tq=256, n_q=n_lat, **dims)
        b_lat = _attn_window(qkv_lat, qkv_ctx, qkv_lat, sink_b[l], n_q=n_lat, **dims)
        c_lat = _attn_global(qkv_lat, qkv_ctx, qkv_lat, diff=True, tq=512, n_q=n_lat, **dims, **diff_args)
        y_lat = _merge(h_lat, a_lat, b_lat, c_lat, w_in_h[l], w_branch_h[l], tm=tm_lat_merge)
        x_lat = _out_project(y_lat, x_lat, mod_lat, g_post[1], w_out_h[l], tm=tm_lat_out, **lat)
        x_lat = _ffn_half_step(x_lat, mod_lat, g_pre[2], g_post[2], w_ff_in_h[l, 1], w_ff_out_h[l, 1],
                               mod_row=6, tm=tm_lat_ffn, **lat)

        if not last:
            a_ctx = _attn_global(qkv_ctx, qkv_ctx, None, diff=False, tq=n_ctx, n_q=n_ctx, **dims)
            b_ctx = _attn_window(qkv_ctx, qkv_ctx, None, sink_b[l], n_q=n_ctx, **dims)
            c_ctx_o = _attn_global(qkv_ctx, qkv_ctx, None, diff=True, tq=n_ctx, n_q=n_ctx, **dims, **diff_args)
            y_ctx = _merge(h_ctx, a_ctx, b_ctx, c_ctx_o, w_in_h[l], w_branch_h[l], tm=tm_ctx)
            x_ctx = _out_project(y_ctx, x_ctx, mod_ctx, g_post[1], w_out_h[l], tm=tm_ctx, **cx)
            x_ctx = _ffn_half_step(x_ctx, mod_ctx, g_pre[2], g_post[2], w_ff_in_h[l, 1], w_ff_out_h[l, 1],
                                   mod_row=6, tm=tm_ctx, **cx)

    return x_lat.reshape(batch, n_lat, D_MODEL)
```

```python
import functools
import math

import jax
import jax.numpy as jnp
from jax import lax
from jax.experimental import pallas as pl
from jax.experimental.pallas import tpu as pltpu

D_MODEL = 2048
DEPTH = 4
GRID_W = 64
HEAD_DIM = 128
A_HEADS = 8
A_KV_HEADS = 2
B_HEADS = 8
B_KV_HEADS = 2
C_HEADS = 8
C_QK_DIM = HEAD_DIM // 2
BRANCH_WIDTH = 1024
N_BRANCH = 3
WINDOW = 128
D_FF = 5632
HALF_STEP = 0.5
ROPE_THETA = 10000.0
NORM_EPS = 1e-6
NEG_INF = -1e30
N_MOD = 9
QKV_WIDTH = 6144
LANES = 128
SUBLANES = 8
VMEM_LIMIT = 56 * 1024 * 1024

A_Q, A_K, A_V = 0, 8, 10
B_Q, B_K, B_V = 12, 20, 22
C_Q, C_K, C_V = 24, 32, 40

F32 = jnp.float32
BF16 = jnp.bfloat16

LOG2_E = math.log2(math.e)
Q_GAIN_AB = HEAD_DIM ** -0.5 * LOG2_E
Q_GAIN_C = C_QK_DIM ** -0.5 * LOG2_E


def _params(sem, vmem=VMEM_LIMIT):
    return pltpu.CompilerParams(dimension_semantics=sem, vmem_limit_bytes=vmem)


def _rms(x, g):
    ms = jnp.mean(x * x, axis=-1, keepdims=True)
    return x * lax.rsqrt(ms + NORM_EPS) * g


def _dot(a, b):
    return jnp.dot(a, b, preferred_element_type=F32)


def _dot_nt(a, b):
    return lax.dot_general(a, b, (((1,), (1,)), ((), ())), preferred_element_type=F32)


MOD_TN = 1024


def _mod_kernel(c_ref, w_ref, b_ref, o_ref):
    n_rep = MOD_TN // LANES

    def body(kc, accs):
        r0 = pl.multiple_of(kc * SUBLANES, SUBLANES)
        wk = w_ref[0, pl.ds(r0, SUBLANES), :]
        out = []
        for m in range(3):
            cm = c_ref[m, pl.ds(r0, SUBLANES), :]
            sm = cm * jax.nn.sigmoid(cm)
            out.append(accs[m] + jnp.concatenate([sm] * n_rep, axis=1) * wk)
        return tuple(out)

    zero = jnp.zeros((SUBLANES, MOD_TN), F32)
    accs = lax.fori_loop(0, D_MODEL // SUBLANES, body, (zero, zero, zero), unroll=4)
    bias = b_ref[0]
    for m in range(3):
        o_ref[0, m:m + 1, :] = jnp.sum(accs[m], axis=0, keepdims=True) + bias
    o_ref[0, 3:SUBLANES, :] = jnp.zeros((SUBLANES - 3, MOD_TN), F32)


def _modulation(cvec, w_ada, b_ada):
    width = N_MOD * D_MODEL
    c_b = jnp.broadcast_to(cvec[:, :, None], (3, D_MODEL, LANES))
    return pl.pallas_call(
        _mod_kernel,
        grid=(DEPTH, width // MOD_TN),
        in_specs=[
            pl.BlockSpec((3, D_MODEL, LANES), lambda l, j: (0, 0, 0)),
            pl.BlockSpec((1, D_MODEL, MOD_TN), lambda l, j: (l, 0, j)),
            pl.BlockSpec((1, 1, MOD_TN), lambda l, j: (l, 0, j)),
        ],
        out_specs=pl.BlockSpec((1, SUBLANES, MOD_TN), lambda l, j: (l, 0, j)),
        out_shape=jax.ShapeDtypeStruct((DEPTH, SUBLANES, width), F32),
        name="adaln_modulation",
        compiler_params=_params(("arbitrary", "arbitrary")),
    )(c_b, w_ada, b_ada.reshape(DEPTH, 1, width))


FFN_TF = 512


def _ffn_kernel(x_ref, mod_ref, gpre_ref, gpost_ref, w1g_ref, w1u_ref, w2_ref, o_ref,
                h_ref, acc_ref, *, mod_row):
    j = pl.program_id(1)

    @pl.when(j == 0)
    def _():
        shift = mod_ref[mod_row:mod_row + 1, :]
        scale = mod_ref[mod_row + 1:mod_row + 2, :]
        h = _rms(x_ref[...], gpre_ref[...]) * (1.0 + scale) + shift
        h_ref[...] = h.astype(BF16)
        acc_ref[...] = jnp.zeros(acc_ref.shape, F32)

    h = h_ref[...]
    g = _dot(h, w1g_ref[...])
    u = _dot(h, w1u_ref[...])
    a = (g * jax.nn.sigmoid(g)) * u
    acc_ref[...] += _dot(a.astype(BF16), w2_ref[...])

    @pl.when(j == pl.num_programs(1) - 1)
    def _():
        gate = mod_ref[mod_row + 2:mod_row + 3, :]
        y = _rms(acc_ref[...], gpost_ref[...])
        o_ref[...] = x_ref[...] + HALF_STEP * gate * y


def _ffn_half_step(x, mod, g_pre, g_post, w1, w2, *, layer, which, mod_row, tm, rows_per_group):
    rows = x.shape[0]
    nf = D_FF // FFN_TF
    tiles_per_group = rows_per_group // tm
    return pl.pallas_call(
        functools.partial(_ffn_kernel, mod_row=mod_row),
        grid=(rows // tm, nf),
        in_specs=[
            pl.BlockSpec((tm, D_MODEL), lambda i, j: (i, 0)),
            pl.BlockSpec((None, N_MOD, D_MODEL), lambda i, j: (i // tiles_per_group, 0, 0)),
            pl.BlockSpec((1, D_MODEL), lambda i, j: (0, 0)),
            pl.BlockSpec((1, D_MODEL), lambda i, j: (0, 0)),
            pl.BlockSpec((None, None, D_MODEL, FFN_TF), lambda i, j: (layer, which, 0, j)),
            pl.BlockSpec((None, None, D_MODEL, FFN_TF), lambda i, j: (layer, which, 0, nf + j)),
            pl.BlockSpec((None, None, FFN_TF, D_MODEL), lambda i, j: (layer, which, j, 0)),
        ],
        out_specs=pl.BlockSpec((tm, D_MODEL), lambda i, j: (i, 0)),
        out_shape=jax.ShapeDtypeStruct((rows, D_MODEL), F32),
        scratch_shapes=[pltpu.VMEM((tm, D_MODEL), BF16), pltpu.VMEM((tm, D_MODEL), F32)],
        name="ffn_half_step",
        compiler_params=_params(("arbitrary", "arbitrary")),
    )(x, mod, g_pre, g_post, w1, w1, w2)


PROJ_TN = 512
PROJ_CHUNKS = PROJ_TN // LANES


def _rope(y, tab_ref, base, quarter):
    cos, sin_up, sin_dn = tab_ref[base], tab_ref[base + 1], tab_ref[base + 2]
    return (y * cos + pltpu.roll(y, LANES - quarter, 1) * sin_up + pltpu.roll(y, quarter, 1) * sin_dn)


def _proj_kernel(*refs, use_rope):
    if use_rope:
        x_ref, mod_ref, gpre_ref, w_ref, qkn_ref, tab_ref, o_ref, ho_ref, h_ref = refs
    else:
        x_ref, mod_ref, gpre_ref, w_ref, qkn_ref, o_ref, ho_ref, h_ref = refs
        tab_ref = None
    j = pl.program_id(1)

    @pl.when(j == 0)
    def _():
        shift = mod_ref[3:4, :]
        scale = mod_ref[4:5, :]
        h = (_rms(x_ref[...], gpre_ref[...]) * (1.0 + scale) + shift).astype(BF16)
        h_ref[...] = h
        ho_ref[...] = h

    y = _dot(h_ref[...], w_ref[...])

    def chunk(c):
        return y[:, c * LANES:(c + 1) * LANES]

    def put(c, v):
        o_ref[:, c * LANES:(c + 1) * LANES] = v.astype(BF16)

    def plain(c):
        put(c, chunk(c))

    def rope_ab(c, gain=None):
        v = chunk(c)
        v = _rope(v, tab_ref, 0, HEAD_DIM // 4) if use_rope else v
        put(c, v if gain is None else v * gain)

    def rope_c(c, gain=None):
        v = chunk(c)
        v = _rope(v, tab_ref, 3, C_QK_DIM // 4) if use_rope else v
        put(c, v if gain is None else v * gain)

    def norm_rope(c, row, gain=None):
        v = _rms(chunk(c), qkn_ref[row:row + 1, :])
        v = _rope(v, tab_ref, 0, HEAD_DIM // 4) if use_rope else v
        put(c, v if gain is None else v * gain)

    def tile_is(lo, hi):
        return jnp.logical_and(j >= lo, j < hi)

    @pl.when(tile_is(0, 2))
    def _():
        for c in range(PROJ_CHUNKS):
            norm_rope(c, 0, Q_GAIN_AB)

    @pl.when(j == 2)
    def _():
        norm_rope(0, 1)
        norm_rope(1, 1)
        plain(2)
        plain(3)

    @pl.when(tile_is(3, 5))
    def _():
        for c in range(PROJ_CHUNKS):
            rope_ab(c, Q_GAIN_AB)

    @pl.when(j == 5)
    def _():
        rope_ab(0)
        rope_ab(1)
        plain(2)
        plain(3)

    @pl.when(tile_is(6, 8))
    def _():
        for c in range(PROJ_CHUNKS):
            rope_c(c, Q_GAIN_C)

    @pl.when(tile_is(8, 10))
    def _():
        for c in range(PROJ_CHUNKS):
            rope_c(c)

    @pl.when(j >= 10)
    def _():
        for c in range(PROJ_CHUNKS):
            plain(c)


def _project(x, mod, g_pre, w_in, qk_norm, tables, *, layer, tm, rows_per_group, n_pos):
    rows = x.shape[0]
    use_rope = tables is not None
    tiles_per_group = rows_per_group // tm
    in_specs = [
        pl.BlockSpec((tm, D_MODEL), lambda i, j: (i, 0)),
        pl.BlockSpec((None, N_MOD, D_MODEL), lambda i, j: (i // tiles_per_group, 0, 0)),
        pl.BlockSpec((1, D_MODEL), lambda i, j: (0, 0)),
        pl.BlockSpec((None, D_MODEL, PROJ_TN), lambda i, j: (layer, 0, j)),
        pl.BlockSpec((2, HEAD_DIM), lambda i, j: (0, 0)),
    ]
    args = [x, mod, g_pre, w_in, qk_norm]
    if use_rope:
        pos_tiles = n_pos // tm
        in_specs.append(pl.BlockSpec((6, tm, LANES), lambda i, j: (0, i % pos_tiles, 0)))
        args.append(tables)
    return pl.pallas_call(
        functools.partial(_proj_kernel, use_rope=use_rope),
        grid=(rows // tm, QKV_WIDTH // PROJ_TN),
        in_specs=in_specs,
        out_specs=[
            pl.BlockSpec((tm, PROJ_TN), lambda i, j: (i, j)),
            pl.BlockSpec((tm, D_MODEL), lambda i, j: (i, 0)),
        ],
        out_shape=[
            jax.ShapeDtypeStruct((rows, QKV_WIDTH), BF16),
            jax.ShapeDtypeStruct((rows, D_MODEL), BF16),
        ],
        scratch_shapes=[pltpu.VMEM((tm, D_MODEL), BF16)],
        name="in_projection",
        compiler_params=_params(("arbitrary", "arbitrary")),
    )(*args)


def _rope_tables(n_tok):
    rows = n_tok // GRID_W
    row = jnp.repeat(jnp.arange(rows, dtype=jnp.int32), GRID_W).astype(F32)
    col = jnp.tile(jnp.arange(GRID_W, dtype=jnp.int32), rows).astype(F32)
    out = []
    for rot_dim in (HEAD_DIM, C_QK_DIM):
        d_ax = rot_dim // 2
        inv = ROPE_THETA ** (-jnp.arange(0, d_ax, 2, dtype=F32) / d_ax)
        ang_r = row[:, None] * inv[None, :]
        ang_c = col[:, None] * inv[None, :]
        ang = jnp.concatenate([ang_r, ang_r, ang_c, ang_c], axis=-1)
        reps = LANES // rot_dim
        cos = jnp.tile(jnp.cos(ang), (1, reps))
        sin = jnp.tile(jnp.sin(ang), (1, reps))
        quarter = (jnp.arange(LANES) % rot_dim) // (rot_dim // 4)
        even = (quarter % 2 == 0)[None, :]
        out += [cos, jnp.where(even, -sin, 0.0), jnp.where(even, 0.0, sin)]
    return jnp.stack(out, axis=0)


ATT_TK = 512
ATT_RB = 128


def _softmax_block(s, v, rows, m_ref, l_ref, acc_ref):
    blocks = [s[:, j * LANES:(j + 1) * LANES] for j in range(s.shape[1] // LANES)]
    m_prev = m_ref[rows, :]
    m_new = jnp.maximum(m_prev, jnp.max(functools.reduce(jnp.maximum, blocks), axis=-1, keepdims=True))
    alpha = jnp.exp2(m_prev - m_new)
    p = [jnp.exp2(b - m_new) for b in blocks]
    l_ref[rows, :] = alpha * l_ref[rows, :] + functools.reduce(jnp.add, p)
    pv = _dot(jnp.concatenate(p, axis=1).astype(BF16), v)
    acc_ref[rows, :] = alpha * acc_ref[rows, :] + pv
    m_ref[rows, :] = m_new


def _attend_chunk(qs_ref, k, v, m_ref, l_ref, acc_ref, mask_fn=None):
    for r in range(qs_ref.shape[0] // ATT_RB):
        rows = slice(r * ATT_RB, (r + 1) * ATT_RB)
        s = _dot_nt(qs_ref[rows, :], k)
        if mask_fn is not None:
            s = mask_fn(s, r * ATT_RB)
        _softmax_block(s, v, rows, m_ref, l_ref, acc_ref)


def _pipelined_keys(qs_ref, kc_ref, vc_ref, k_ref, v_ref, m_ref, l_ref, acc_ref, pipe_refs, n_chunks):
    s_refs, p_refs, a_refs = pipe_refs[0:2], pipe_refs[2:4], pipe_refs[4:6]
    n_rows = qs_ref.shape[0]
    n_ctx = kc_ref.shape[0]
    assert n_chunks % 2 == 0 and n_ctx <= ATT_TK

    def chunk(ref, c):
        return ref[pl.ds(pl.multiple_of(c * ATT_TK, ATT_TK), ATT_TK), :]

    def scores(k, buf):
        s_refs[buf][:, 0:k.shape[0]] = _dot_nt(qs_ref[...], k)

    def softmax(buf, width=ATT_TK):
        for r in range(n_rows // ATT_RB):
            rows = slice(r * ATT_RB, (r + 1) * ATT_RB)
            s = s_refs[buf][rows, 0:width]
            blocks = [s[:, j * LANES:(j + 1) * LANES] for j in range(width // LANES)]
            m_prev = m_ref[rows, :]
            m_new = jnp.maximum(
                m_prev, jnp.max(functools.reduce(jnp.maximum, blocks), axis=-1, keepdims=True))
            alpha = jnp.exp2(m_prev - m_new)
            p = [jnp.exp2(b - m_new) for b in blocks]
            l_ref[rows, :] = alpha * l_ref[rows, :] + functools.reduce(jnp.add, p)
            m_ref[rows, :] = m_new
            a_refs[buf][rows, :] = alpha
            p_refs[buf][rows, 0:width] = jnp.concatenate(p, axis=1).astype(BF16)

    def values(v, buf):
        pv = _dot(p_refs[buf][:, 0:v.shape[0]], v)
        acc_ref[...] = a_refs[buf][...] * acc_ref[...] + pv

    scores(kc_ref[...], 0)
    softmax(0, n_ctx)
    scores(chunk(k_ref, 0), 1)
    values(vc_ref[...], 0)
    softmax(1)
    scores(chunk(k_ref, 1), 0)

    def body(i, carry):
        c = 2 * i + 1
        values(chunk(v_ref, c - 1), 1)
        softmax(0)
        scores(chunk(k_ref, c + 1), 1)
        values(chunk(v_ref, c), 0)
        softmax(1)
        scores(chunk(k_ref, c + 2), 0)
        return carry

    lax.fori_loop(0, n_chunks // 2 - 1, body, 0)
    values(chunk(v_ref, n_chunks - 2), 1)
    softmax(0)
    values(chunk(v_ref, n_chunks - 1), 0)


def _attend_result(l_ref, acc_ref):
    return acc_ref[...] / jnp.sum(l_ref[...], axis=-1, keepdims=True)


def _attn_global_kernel(*refs, diff, with_lat, tq, n_lat, lambda_init):
    refs = list(refs)
    q_ref, kc_ref, vc_ref = refs[:3]
    pos = 3
    if with_lat:
        kl_ref, vl_ref = refs[pos:pos + 2]
        pos += 2
    if diff:
        lam_ref, subln_ref = refs[pos:pos + 2]
        pos += 2
    o_ref, qs_ref, m_ref, l_ref, acc_ref = refs[pos:pos + 5]
    pipe_refs = refs[pos + 5:]

    if diff:
        q = q_ref[...]
        lane = lax.broadcasted_iota(jnp.int32, q.shape, 1)
        qs_ref[0:tq, :] = jnp.where(lane < C_QK_DIM, q, jnp.zeros_like(q))
        qs_ref[tq:2 * tq, :] = jnp.where(lane >= C_QK_DIM, q, jnp.zeros_like(q))
        n_stack = 2
    else:
        n_stack = A_HEADS // A_KV_HEADS
        for g in range(n_stack):
            qs_ref[g * tq:(g + 1) * tq, :] = q_ref[:, g * HEAD_DIM:(g + 1) * HEAD_DIM]

    m_ref[...] = jnp.full(m_ref.shape, NEG_INF, F32)
    l_ref[...] = jnp.zeros(l_ref.shape, F32)
    acc_ref[...] = jnp.zeros(acc_ref.shape, F32)

    if with_lat:
        _pipelined_keys(qs_ref, kc_ref, vc_ref, kl_ref, vl_ref, m_ref, l_ref, acc_ref, pipe_refs,
                        n_lat // ATT_TK)
    else:
        _attend_chunk(qs_ref, kc_ref[...], vc_ref[...], m_ref, l_ref, acc_ref)

    o = _attend_result(l_ref, acc_ref)
    if diff:
        lf = lam_ref[...]
        lam = (jnp.exp(jnp.sum(lf[0:1] * lf[1:2], axis=-1, keepdims=True))
               - jnp.exp(jnp.sum(lf[2:3] * lf[3:4], axis=-1, keepdims=True)) + lambda_init)
        d = o[0:tq] - lam * o[tq:2 * tq]
        o_ref[...] = (_rms(d, subln_ref[...]) * (1.0 - lambda_init)).astype(BF16)
    else:
        for g in range(n_stack):
            o_ref[:, g * HEAD_DIM:(g + 1) * HEAD_DIM] = o[g * tq:(g + 1) * tq].astype(BF16)


def _attn_global(q_src, qkv_ctx, qkv_lat, *, diff, tq, n_q, n_ctx, n_lat, batch,
                 lam=None, subln=None, lambda_init=0.0):
    with_lat = qkv_lat is not None
    q_tiles = n_q // tq
    if diff:
        heads, n_stack, q_w = C_HEADS, 2, HEAD_DIM
        q_col, k_col, v_col = C_Q, C_K, C_V
    else:
        heads, n_stack, q_w = A_KV_HEADS, A_HEADS // A_KV_HEADS, (A_HEADS // A_KV_HEADS) * HEAD_DIM
        q_col, k_col, v_col = A_Q * LANES // q_w, A_K, A_V
    in_specs = [
        pl.BlockSpec((tq, q_w), lambda b, h, t: (b * q_tiles + t, q_col + h)),
        pl.BlockSpec((n_ctx, HEAD_DIM), lambda b, h, t: (b, k_col + h)),
        pl.BlockSpec((n_ctx, HEAD_DIM), lambda b, h, t: (b, v_col + h)),
    ]
    args = [q_src, qkv_ctx, qkv_ctx]
    if with_lat:
        in_specs += [
            pl.BlockSpec((n_lat, HEAD_DIM), lambda b, h, t: (b, k_col + h)),
            pl.BlockSpec((n_lat, HEAD_DIM), lambda b, h, t: (b, v_col + h)),
        ]
        args += [qkv_lat, qkv_lat]
    if diff:
        in_specs += [
            pl.BlockSpec((4, C_QK_DIM), lambda b, h, t: (0, 0)),
            pl.BlockSpec((1, HEAD_DIM), lambda b, h, t: (0, 0)),
        ]
        args += [lam, subln]
    rows = n_stack * tq
    scratch = [
        pltpu.VMEM((rows, HEAD_DIM), BF16),
        pltpu.VMEM((rows, LANES), F32),
        pltpu.VMEM((rows, LANES), F32),
        pltpu.VMEM((rows, HEAD_DIM), F32),
    ]
    if with_lat:
        scratch += ([pltpu.VMEM((rows, ATT_TK), F32)] * 2 + [pltpu.VMEM((rows, ATT_TK), BF16)] * 2
                    + [pltpu.VMEM((rows, LANES), F32)] * 2)
    return pl.pallas_call(
        functools.partial(_attn_global_kernel, diff=diff, with_lat=with_lat, tq=tq, n_lat=n_lat,
                          lambda_init=lambda_init),
        grid=(batch, heads, q_tiles),
        in_specs=in_specs,
        out_specs=pl.BlockSpec((tq, q_w), lambda b, h, t: (b * q_tiles + t, h)),
        out_shape=jax.ShapeDtypeStruct((batch * n_q, BRANCH_WIDTH), BF16),
        scratch_shapes=scratch,
        name="attn_diff" if diff else "attn_global",
        compiler_params=_params(("arbitrary", "arbitrary", "arbitrary")),
    )(*args)


WIN_TQ = 256
WIN_SPAN = WIN_TQ + 2 * WINDOW


def _attn_window_kernel(*refs, with_lat, n_lat):
    if with_lat:
        sink_ref, q_ref, kc_ref, vc_ref, kl_ref, vl_ref, o_ref, qs_ref, m_ref, l_ref, acc_ref = refs
    else:
        sink_ref, q_ref, kc_ref, vc_ref, o_ref, qs_ref, m_ref, l_ref, acc_ref = refs
    tq = WIN_TQ
    n_stack = B_HEADS // B_KV_HEADS
    h = pl.program_id(1)
    t = pl.program_id(2)
    for g in range(n_stack):
        qs_ref[g * tq:(g + 1) * tq, :] = q_ref[:, g * HEAD_DIM:(g + 1) * HEAD_DIM]
        m_ref[g * tq:(g + 1) * tq, :] = jnp.full((tq, LANES), sink_ref[h * n_stack + g] * LOG2_E, F32)
    l_ref[...] = jnp.full(l_ref.shape, 1.0 / LANES, F32)
    acc_ref[...] = jnp.zeros(acc_ref.shape, F32)

    _attend_chunk(qs_ref, kc_ref[...], vc_ref[...], m_ref, l_ref, acc_ref)

    if with_lat:
        start = jnp.clip(t * tq - WINDOW, 0, n_lat - WIN_SPAN)
        start = pl.multiple_of(start, WINDOW)

        def band(s, row0):
            row = lax.broadcasted_iota(jnp.int32, s.shape, 0)
            col = lax.broadcasted_iota(jnp.int32, s.shape, 1)
            qpos = t * tq + row0 % tq + row
            kpos = start + col
            return jnp.where(jnp.abs(kpos - qpos) <= WINDOW, s, NEG_INF)

        _attend_chunk(qs_ref, kl_ref[pl.ds(start, WIN_SPAN), :], vl_ref[pl.ds(start, WIN_SPAN), :],
                      m_ref, l_ref, acc_ref, mask_fn=band)

    o = _attend_result(l_ref, acc_ref)
    for g in range(n_stack):
        o_ref[:, g * HEAD_DIM:(g + 1) * HEAD_DIM] = o[g * tq:(g + 1) * tq].astype(BF16)


def _attn_window(q_src, qkv_ctx, qkv_lat, sink, *, n_q, n_ctx, n_lat, batch):
    with_lat = qkv_lat is not None
    tq = WIN_TQ
    q_tiles = n_q // tq
    n_stack = B_HEADS // B_KV_HEADS
    q_w = n_stack * HEAD_DIM
    q_col = B_Q * LANES // q_w
    in_specs = [
        pl.BlockSpec(memory_space=pltpu.SMEM),
        pl.BlockSpec((tq, q_w), lambda b, h, t: (b * q_tiles + t, q_col + h)),
        pl.BlockSpec((n_ctx, HEAD_DIM), lambda b, h, t: (b, B_K + h)),
        pl.BlockSpec((n_ctx, HEAD_DIM), lambda b, h, t: (b, B_V + h)),
    ]
    args = [sink, q_src, qkv_ctx, qkv_ctx]
    if with_lat:
        in_specs += [
            pl.BlockSpec((n_lat, HEAD_DIM), lambda b, h, t: (b, B_K + h)),
            pl.BlockSpec((n_lat, HEAD_DIM), lambda b, h, t: (b, B_V + h)),
        ]
        args += [qkv_lat, qkv_lat]
    rows = n_stack * tq
    return pl.pallas_call(
        functools.partial(_attn_window_kernel, with_lat=with_lat, n_lat=n_lat),
        grid=(batch, B_KV_HEADS, q_tiles),
        in_specs=in_specs,
        out_specs=pl.BlockSpec((tq, q_w), lambda b, h, t: (b * q_tiles + t, h)),
        out_shape=jax.ShapeDtypeStruct((batch * n_q, BRANCH_WIDTH), BF16),
        scratch_shapes=[
            pltpu.VMEM((rows, HEAD_DIM), BF16),
            pltpu.VMEM((rows, LANES), F32),
            pltpu.VMEM((rows, LANES), F32),
            pltpu.VMEM((rows, HEAD_DIM), F32),
        ],
        name="attn_window",
        compiler_params=_params(("arbitrary", "arbitrary", "arbitrary")),
    )(*args)


MERGE_TN = 1024


def _merge_kernel(h_ref, oa_ref, ob_ref, oc_ref, wg_ref, wb_ref, y_ref, acc_ref):
    r = pl.program_id(2)
    gate = jax.nn.sigmoid(_dot(h_ref[...], wg_ref[...]))
    for k, o_ref in enumerate((oa_ref, ob_ref, oc_ref)):
        @pl.when(r == k)
        def _(o_ref=o_ref, k=k):
            part = gate * _dot(o_ref[...], wb_ref[...])
            if k == 0:
                acc_ref[...] = part
            elif k < N_BRANCH - 1:
                acc_ref[...] += part
            else:
                y_ref[...] = (acc_ref[...] + part).astype(BF16)


def _merge(h, o_a, o_b, o_c, w_in, w_branch, *, layer, tm):
    rows = h.shape[0]
    n_col = D_MODEL // MERGE_TN
    gate_col = QKV_WIDTH // MERGE_TN
    o_spec = pl.BlockSpec((tm, BRANCH_WIDTH), lambda i, n, r: (i, 0))
    return pl.pallas_call(
        _merge_kernel,
        grid=(rows // tm, n_col, N_BRANCH),
        in_specs=[
            pl.BlockSpec((tm, D_MODEL), lambda i, n, r: (i, 0)),
            o_spec, o_spec, o_spec,
            pl.BlockSpec((None, D_MODEL, MERGE_TN), lambda i, n, r: (layer, 0, gate_col + r * n_col + n)),
            pl.BlockSpec((None, None, BRANCH_WIDTH, MERGE_TN), lambda i, n, r: (layer, r, 0, n)),
        ],
        out_specs=pl.BlockSpec((tm, MERGE_TN), lambda i, n, r: (i, n)),
        out_shape=jax.ShapeDtypeStruct((rows, D_MODEL), BF16),
        scratch_shapes=[pltpu.VMEM((tm, MERGE_TN), F32)],
        name="branch_merge",
        compiler_params=_params(("arbitrary", "arbitrary", "arbitrary")),
    )(h, o_a, o_b, o_c, w_in, w_branch)


def _out_kernel(y_ref, x_ref, mod_ref, gpost_ref, w_ref, o_ref):
    z = _dot(y_ref[...], w_ref[...])
    o_ref[...] = x_ref[...] + mod_ref[5:6, :] * _rms(z, gpost_ref[...])


def _out_project(y, x, mod, g_post, w_out, *, layer, tm, rows_per_group):
    rows = x.shape[0]
    tiles_per_group = rows_per_group // tm
    return pl.pallas_call(
        _out_kernel,
        grid=(rows // tm,),
        in_specs=[
            pl.BlockSpec((tm, D_MODEL), lambda i: (i, 0)),
            pl.BlockSpec((tm, D_MODEL), lambda i: (i, 0)),
            pl.BlockSpec((None, N_MOD, D_MODEL), lambda i: (i // tiles_per_group, 0, 0)),
            pl.BlockSpec((1, D_MODEL), lambda i: (0, 0)),
            pl.BlockSpec((None, D_MODEL, D_MODEL), lambda i: (layer, 0, 0)),
        ],
        out_specs=pl.BlockSpec((tm, D_MODEL), lambda i: (i, 0)),
        out_shape=jax.ShapeDtypeStruct((rows, D_MODEL), F32),
        name="out_projection",
        compiler_params=_params(("arbitrary",)),
    )(y, x, mod, g_post, w_out)


def kernel(x, c, ctx, c_ctx, w_ada, b_ada, norm_pre, norm_post, w_ff_in, w_ff_out, w_in, qk_norm_a,
           sink_b, lam_c, subln_c, w_branch, w_out):
    batch, n_lat, _ = x.shape
    n_ctx = ctx.shape[1]
    depth = w_ada.shape[0]
    assert (depth, x.shape[2]) == (DEPTH, D_MODEL)

    tm_lat_ffn, tm_lat_proj, tm_lat_merge, tm_lat_out = 512, 1024, 1024, 512
    tm_ctx = batch * n_ctx

    cvec = jnp.concatenate([c, c_ctx[None, :]], axis=0)
    mod_all = _modulation(cvec, w_ada, b_ada)
    tables = _rope_tables(n_lat)

    w_ff_in_h = w_ff_in.astype(BF16)
    w_ff_out_h = w_ff_out.astype(BF16)
    w_in_h = w_in.astype(BF16)
    w_branch_h = w_branch.astype(BF16)
    w_out_h = w_out.astype(BF16)

    x_lat = x.reshape(batch * n_lat, D_MODEL)
    x_ctx = ctx.reshape(batch * n_ctx, D_MODEL)

    for l in range(depth):
        last = l == depth - 1
        lambda_init = 0.8 - 0.6 * math.exp(-0.3 * l)
        mod_lat = mod_all[l, 0:batch].reshape(batch, N_MOD, D_MODEL)
        mod_ctx = mod_all[l, batch:batch + 1].reshape(1, N_MOD, D_MODEL)
        g_pre = [norm_pre[l, s][None, :] for s in range(3)]
        g_post = [norm_post[l, s][None, :] for s in range(3)]
        lat = dict(rows_per_group=n_lat)
        cx = dict(rows_per_group=tm_ctx)

        ffn1 = dict(layer=l, which=0, mod_row=0)
        ffn2 = dict(layer=l, which=1, mod_row=6)
        x_ctx = _ffn_half_step(x_ctx, mod_ctx, g_pre[0], g_post[0], w_ff_in_h, w_ff_out_h,
                               tm=tm_ctx, **ffn1, **cx)
        x_lat = _ffn_half_step(x_lat, mod_lat, g_pre[0], g_post[0], w_ff_in_h, w_ff_out_h,
                               tm=tm_lat_ffn, **ffn1, **lat)

        qkv_ctx, h_ctx = _project(x_ctx, mod_ctx, g_pre[1], w_in_h, qk_norm_a[l], None,
                                  layer=l, tm=tm_ctx, n_pos=n_lat, **cx)
        qkv_lat, h_lat = _project(x_lat, mod_lat, g_pre[1], w_in_h, qk_norm_a[l], tables,
                                  layer=l, tm=tm_lat_proj, n_pos=n_lat, **lat)

        dims = dict(n_ctx=n_ctx, n_lat=n_lat, batch=batch)
        diff_args = dict(lam=lam_c[l], subln=subln_c[l][None, :], lambda_init=lambda_init)
        a_lat = _attn_global(qkv_lat, qkv_ctx, qkv_lat, diff=False, tq=256, n_q=n_lat, **dims)
        b_lat = _attn_window(qkv_lat, qkv_ctx, qkv_lat, sink_b[l], n_q=n_lat, **dims)
        c_lat = _attn_global(qkv_lat, qkv_ctx, qkv_lat, diff=True, tq=512, n_q=n_lat, **dims, **diff_args)
        y_lat = _merge(h_lat, a_lat, b_lat, c_lat, w_in_h, w_branch_h, layer=l, tm=tm_lat_merge)
        x_lat = _out_project(y_lat, x_lat, mod_lat, g_post[1], w_out_h, layer=l, tm=tm_lat_out, **lat)
        x_lat = _ffn_half_step(x_lat, mod_lat, g_pre[2], g_post[2], w_ff_in_h, w_ff_out_h,
                               tm=tm_lat_ffn, **ffn2, **lat)

        if not last:
            a_ctx = _attn_global(qkv_ctx, qkv_ctx, None, diff=False, tq=n_ctx, n_q=n_ctx, **dims)
            b_ctx = _attn_window(qkv_ctx, qkv_ctx, None, sink_b[l], n_q=n_ctx, **dims)
            c_ctx_o = _attn_global(qkv_ctx, qkv_ctx, None, diff=True, tq=n_ctx, n_q=n_ctx, **dims, **diff_args)
            y_ctx = _merge(h_ctx, a_ctx, b_ctx, c_ctx_o, w_in_h, w_branch_h, layer=l, tm=tm_ctx)
            x_ctx = _out_project(y_ctx, x_ctx, mod_ctx, g_post[1], w_out_h, layer=l, tm=tm_ctx, **cx)
            x_ctx = _ffn_half_step(x_ctx, mod_ctx, g_pre[2], g_post[2], w_ff_in_h, w_ff_out_h,
                                   tm=tm_ctx, **ffn2, **cx)

    return x_lat.reshape(batch, n_lat, D_MODEL)
```

```python
import functools
import math

import jax
import jax.numpy as jnp
from jax import lax
from jax.experimental import pallas as pl
from jax.experimental.pallas import tpu as pltpu

D_MODEL = 2048
DEPTH = 4
GRID_W = 64
HEAD_DIM = 128
A_HEADS = 8
A_KV_HEADS = 2
B_HEADS = 8
B_KV_HEADS = 2
C_HEADS = 8
C_QK_DIM = HEAD_DIM // 2
BRANCH_WIDTH = 1024
N_BRANCH = 3
WINDOW = 128
D_FF = 5632
HALF_STEP = 0.5
ROPE_THETA = 10000.0
NORM_EPS = 1e-6
NEG_INF = -1e30
N_MOD = 9
QKV_WIDTH = 6144
LANES = 128
SUBLANES = 8
VMEM_LIMIT = 56 * 1024 * 1024

A_Q, A_K, A_V = 0, 8, 10
B_Q, B_K, B_V = 12, 20, 22
C_Q, C_K, C_V = 24, 32, 40

F32 = jnp.float32
BF16 = jnp.bfloat16

LOG2_E = math.log2(math.e)
Q_GAIN_AB = HEAD_DIM ** -0.5 * LOG2_E
Q_GAIN_C = C_QK_DIM ** -0.5 * LOG2_E


def _params(sem, vmem=VMEM_LIMIT):
    return pltpu.CompilerParams(dimension_semantics=sem, vmem_limit_bytes=vmem)


def _rms(x, g):
    ms = jnp.mean(x * x, axis=-1, keepdims=True)
    return x * lax.rsqrt(ms + NORM_EPS) * g


def _dot(a, b):
    return jnp.dot(a, b, preferred_element_type=F32)


def _dot_nt(a, b):
    return lax.dot_general(a, b, (((1,), (1,)), ((), ())), preferred_element_type=F32)


MOD_TN = 1024


def _mod_kernel(c_ref, w_ref, b_ref, o_ref):
    n_rep = MOD_TN // LANES

    def body(kc, accs):
        r0 = pl.multiple_of(kc * SUBLANES, SUBLANES)
        wk = w_ref[0, pl.ds(r0, SUBLANES), :]
        out = []
        for m in range(3):
            cm = c_ref[m, pl.ds(r0, SUBLANES), :]
            sm = cm * jax.nn.sigmoid(cm)
            out.append(accs[m] + jnp.concatenate([sm] * n_rep, axis=1) * wk)
        return tuple(out)

    zero = jnp.zeros((SUBLANES, MOD_TN), F32)
    accs = lax.fori_loop(0, D_MODEL // SUBLANES, body, (zero, zero, zero), unroll=4)
    bias = b_ref[0]
    for m in range(3):
        o_ref[0, m:m + 1, :] = jnp.sum(accs[m], axis=0, keepdims=True) + bias
    o_ref[0, 3:SUBLANES, :] = jnp.zeros((SUBLANES - 3, MOD_TN), F32)


def _modulation(cvec, w_ada, b_ada):
    width = N_MOD * D_MODEL
    c_b = jnp.broadcast_to(cvec[:, :, None], (3, D_MODEL, LANES))
    return pl.pallas_call(
        _mod_kernel,
        grid=(DEPTH, width // MOD_TN),
        in_specs=[
            pl.BlockSpec((3, D_MODEL, LANES), lambda l, j: (0, 0, 0)),
            pl.BlockSpec((1, D_MODEL, MOD_TN), lambda l, j: (l, 0, j)),
            pl.BlockSpec((1, 1, MOD_TN), lambda l, j: (l, 0, j)),
        ],
        out_specs=pl.BlockSpec((1, SUBLANES, MOD_TN), lambda l, j: (l, 0, j)),
        out_shape=jax.ShapeDtypeStruct((DEPTH, SUBLANES, width), F32),
        name="adaln_modulation",
        compiler_params=_params(("arbitrary", "arbitrary")),
    )(c_b, w_ada, b_ada.reshape(DEPTH, 1, width))


FFN_TF = 512


def _ffn_kernel(x_ref, mod_ref, gpre_ref, gpost_ref, w1g_ref, w1u_ref, w2_ref, o_ref,
                h_ref, acc_ref, *, mod_row):
    j = pl.program_id(1)

    @pl.when(j == 0)
    def _():
        shift = mod_ref[mod_row:mod_row + 1, :]
        scale = mod_ref[mod_row + 1:mod_row + 2, :]
        h = _rms(x_ref[...], gpre_ref[...]) * (1.0 + scale) + shift
        h_ref[...] = h.astype(BF16)
        acc_ref[...] = jnp.zeros(acc_ref.shape, F32)

    h = h_ref[...]
    g = _dot(h, w1g_ref[...])
    u = _dot(h, w1u_ref[...])
    a = (g * jax.nn.sigmoid(g)) * u
    acc_ref[...] += _dot(a.astype(BF16), w2_ref[...])

    @pl.when(j == pl.num_programs(1) - 1)
    def _():
        gate = mod_ref[mod_row + 2:mod_row + 3, :]
        y = _rms(acc_ref[...], gpost_ref[...])
        o_ref[...] = x_ref[...] + HALF_STEP * gate * y


def _ffn_half_step(x, mod, g_pre, g_post, w1, w2, *, layer, which, mod_row, tm, rows_per_group):
    rows = x.shape[0]
    nf = D_FF // FFN_TF
    tiles_per_group = rows_per_group // tm
    return pl.pallas_call(
        functools.partial(_ffn_kernel, mod_row=mod_row),
        grid=(rows // tm, nf),
        in_specs=[
            pl.BlockSpec((tm, D_MODEL), lambda i, j: (i, 0)),
            pl.BlockSpec((None, N_MOD, D_MODEL), lambda i, j: (i // tiles_per_group, 0, 0)),
            pl.BlockSpec((1, D_MODEL), lambda i, j: (0, 0)),
            pl.BlockSpec((1, D_MODEL), lambda i, j: (0, 0)),
            pl.BlockSpec((None, None, D_MODEL, FFN_TF), lambda i, j: (layer, which, 0, j)),
            pl.BlockSpec((None, None, D_MODEL, FFN_TF), lambda i, j: (layer, which, 0, nf + j)),
            pl.BlockSpec((None, None, FFN_TF, D_MODEL), lambda i, j: (layer, which, j, 0)),
        ],
        out_specs=pl.BlockSpec((tm, D_MODEL), lambda i, j: (i, 0)),
        out_shape=jax.ShapeDtypeStruct((rows, D_MODEL), F32),
        scratch_shapes=[pltpu.VMEM((tm, D_MODEL), BF16), pltpu.VMEM((tm, D_MODEL), F32)],
        name="ffn_half_step",
        compiler_params=_params(("arbitrary", "arbitrary")),
    )(x, mod, g_pre, g_post, w1, w1, w2)


PROJ_TN = 512
PROJ_CHUNKS = PROJ_TN // LANES


def _rope(y, tab_ref, base, quarter):
    cos, sin_up, sin_dn = tab_ref[base], tab_ref[base + 1], tab_ref[base + 2]
    return (y * cos + pltpu.roll(y, LANES - quarter, 1) * sin_up + pltpu.roll(y, quarter, 1) * sin_dn)


def _proj_kernel(*refs, use_rope):
    if use_rope:
        x_ref, mod_ref, gpre_ref, w_ref, qkn_ref, tab_ref, o_ref, ho_ref, h_ref = refs
    else:
        x_ref, mod_ref, gpre_ref, w_ref, qkn_ref, o_ref, ho_ref, h_ref = refs
        tab_ref = None
    j = pl.program_id(1)

    @pl.when(j == 0)
    def _():
        shift = mod_ref[3:4, :]
        scale = mod_ref[4:5, :]
        h = (_rms(x_ref[...], gpre_ref[...]) * (1.0 + scale) + shift).astype(BF16)
        h_ref[...] = h
        ho_ref[...] = h

    y = _dot(h_ref[...], w_ref[...])

    def chunk(c):
        return y[:, c * LANES:(c + 1) * LANES]

    def put(c, v):
        o_ref[:, c * LANES:(c + 1) * LANES] = v.astype(BF16)

    def plain(c):
        put(c, chunk(c))

    def rope_ab(c, gain=None):
        v = chunk(c)
        v = _rope(v, tab_ref, 0, HEAD_DIM // 4) if use_rope else v
        put(c, v if gain is None else v * gain)

    def rope_c(c, gain=None):
        v = chunk(c)
        v = _rope(v, tab_ref, 3, C_QK_DIM // 4) if use_rope else v
        put(c, v if gain is None else v * gain)

    def norm_rope(c, row, gain=None):
        v = _rms(chunk(c), qkn_ref[row:row + 1, :])
        v = _rope(v, tab_ref, 0, HEAD_DIM // 4) if use_rope else v
        put(c, v if gain is None else v * gain)

    def tile_is(lo, hi):
        return jnp.logical_and(j >= lo, j < hi)

    @pl.when(tile_is(0, 2))
    def _():
        for c in range(PROJ_CHUNKS):
            norm_rope(c, 0, Q_GAIN_AB)

    @pl.when(j == 2)
    def _():
        norm_rope(0, 1)
        norm_rope(1, 1)
        plain(2)
        plain(3)

    @pl.when(tile_is(3, 5))
    def _():
        for c in range(PROJ_CHUNKS):
            rope_ab(c, Q_GAIN_AB)

    @pl.when(j == 5)
    def _():
        rope_ab(0)
        rope_ab(1)
        plain(2)
        plain(3)

    @pl.when(tile_is(6, 8))
    def _():
        for c in range(PROJ_CHUNKS):
            rope_c(c, Q_GAIN_C)

    @pl.when(tile_is(8, 10))
    def _():
        for c in range(PROJ_CHUNKS):
            rope_c(c)

    @pl.when(j >= 10)
    def _():
        for c in range(PROJ_CHUNKS):
            plain(c)


def _project(x, mod, g_pre, w_in, qk_norm, tables, *, layer, tm, rows_per_group, n_pos):
    rows = x.shape[0]
    use_rope = tables is not None
    tiles_per_group = rows_per_group // tm
    in_specs = [
        pl.BlockSpec((tm, D_MODEL), lambda i, j: (i, 0)),
        pl.BlockSpec((None, N_MOD, D_MODEL), lambda i, j: (i // tiles_per_group, 0, 0)),
        pl.BlockSpec((1, D_MODEL), lambda i, j: (0, 0)),
        pl.BlockSpec((None, D_MODEL, PROJ_TN), lambda i, j: (layer, 0, j)),
        pl.BlockSpec((2, HEAD_DIM), lambda i, j: (0, 0)),
    ]
    args = [x, mod, g_pre, w_in, qk_norm]
    if use_rope:
        pos_tiles = n_pos // tm
        in_specs.append(pl.BlockSpec((6, tm, LANES), lambda i, j: (0, i % pos_tiles, 0)))
        args.append(tables)
    return pl.pallas_call(
        functools.partial(_proj_kernel, use_rope=use_rope),
        grid=(rows // tm, QKV_WIDTH // PROJ_TN),
        in_specs=in_specs,
        out_specs=[
            pl.BlockSpec((tm, PROJ_TN), lambda i, j: (i, j)),
            pl.BlockSpec((tm, D_MODEL), lambda i, j: (i, 0)),
        ],
        out_shape=[
            jax.ShapeDtypeStruct((rows, QKV_WIDTH), BF16),
            jax.ShapeDtypeStruct((rows, D_MODEL), BF16),
        ],
        scratch_shapes=[pltpu.VMEM((tm, D_MODEL), BF16)],
        name="in_projection",
        compiler_params=_params(("arbitrary", "arbitrary")),
    )(*args)


def _rope_tables(n_tok):
    rows = n_tok // GRID_W
    row = jnp.repeat(jnp.arange(rows, dtype=jnp.int32), GRID_W).astype(F32)
    col = jnp.tile(jnp.arange(GRID_W, dtype=jnp.int32), rows).astype(F32)
    out = []
    for rot_dim in (HEAD_DIM, C_QK_DIM):
        d_ax = rot_dim // 2
        inv = ROPE_THETA ** (-jnp.arange(0, d_ax, 2, dtype=F32) / d_ax)
        ang_r = row[:, None] * inv[None, :]
        ang_c = col[:, None] * inv[None, :]
        ang = jnp.concatenate([ang_r, ang_r, ang_c, ang_c], axis=-1)
        reps = LANES // rot_dim
        cos = jnp.tile(jnp.cos(ang), (1, reps))
        sin = jnp.tile(jnp.sin(ang), (1, reps))
        quarter = (jnp.arange(LANES) % rot_dim) // (rot_dim // 4)
        even = (quarter % 2 == 0)[None, :]
        out += [cos, jnp.where(even, -sin, 0.0), jnp.where(even, 0.0, sin)]
    return jnp.stack(out, axis=0)


ATT_TK = 512
ATT_RB = 128


def _softmax_block(s, v, rows, m_ref, l_ref, acc_ref):
    blocks = [s[:, j * LANES:(j + 1) * LANES] for j in range(s.shape[1] // LANES)]
    m_prev = m_ref[rows, :]
    m_new = jnp.maximum(m_prev, jnp.max(functools.reduce(jnp.maximum, blocks), axis=-1, keepdims=True))
    alpha = jnp.exp2(m_prev - m_new)
    p = [jnp.exp2(b - m_new) for b in blocks]
    l_ref[rows, :] = alpha * l_ref[rows, :] + functools.reduce(jnp.add, p)
    pv = _dot(jnp.concatenate(p, axis=1).astype(BF16), v)
    acc_ref[rows, :] = alpha * acc_ref[rows, :] + pv
    m_ref[rows, :] = m_new


def _attend_chunk(qs_ref, k, v, m_ref, l_ref, acc_ref, mask_fn=None):
    for r in range(qs_ref.shape[0] // ATT_RB):
        rows = slice(r * ATT_RB, (r + 1) * ATT_RB)
        s = _dot_nt(qs_ref[rows, :], k)
        if mask_fn is not None:
            s = mask_fn(s, r * ATT_RB)
        _softmax_block(s, v, rows, m_ref, l_ref, acc_ref)


def _pipelined_keys(qs_ref, kc_ref, vc_ref, k_ref, v_ref, m_ref, l_ref, acc_ref, pipe_refs, n_chunks):
    s_refs, p_refs, a_refs = pipe_refs[0:2], pipe_refs[2:4], pipe_refs[4:6]
    n_rows = qs_ref.shape[0]
    n_ctx = kc_ref.shape[0]
    assert n_chunks % 2 == 0 and n_ctx <= ATT_TK

    def chunk(ref, c):
        return ref[pl.ds(pl.multiple_of(c * ATT_TK, ATT_TK), ATT_TK), :]

    def scores(k, buf):
        s_refs[buf][:, 0:k.shape[0]] = _dot_nt(qs_ref[...], k)

    def softmax(buf, width=ATT_TK):
        for r in range(n_rows // ATT_RB):
            rows = slice(r * ATT_RB, (r + 1) * ATT_RB)
            s = s_refs[buf][rows, 0:width]
            blocks = [s[:, j * LANES:(j + 1) * LANES] for j in range(width // LANES)]
            m_prev = m_ref[rows, :]
            m_new = jnp.maximum(
                m_prev, jnp.max(functools.reduce(jnp.maximum, blocks), axis=-1, keepdims=True))
            alpha = jnp.exp2(m_prev - m_new)
            p = [jnp.exp2(b - m_new) for b in blocks]
            l_ref[rows, :] = alpha * l_ref[rows, :] + functools.reduce(jnp.add, p)
            m_ref[rows, :] = m_new
            a_refs[buf][rows, :] = alpha
            p_refs[buf][rows, 0:width] = jnp.concatenate(p, axis=1).astype(BF16)

    def values(v, buf):
        pv = _dot(p_refs[buf][:, 0:v.shape[0]], v)
        acc_ref[...] = a_refs[buf][...] * acc_ref[...] + pv

    scores(kc_ref[...], 0)
    softmax(0, n_ctx)
    scores(chunk(k_ref, 0), 1)
    values(vc_ref[...], 0)
    softmax(1)
    scores(chunk(k_ref, 1), 0)

    def body(i, carry):
        c = 2 * i + 1
        values(chunk(v_ref, c - 1), 1)
        softmax(0)
        scores(chunk(k_ref, c + 1), 1)
        values(chunk(v_ref, c), 0)
        softmax(1)
        scores(chunk(k_ref, c + 2), 0)
        return carry

    lax.fori_loop(0, n_chunks // 2 - 1, body, 0, unroll=True)
    values(chunk(v_ref, n_chunks - 2), 1)
    softmax(0)
    values(chunk(v_ref, n_chunks - 1), 0)


def _attend_result(l_ref, acc_ref):
    return acc_ref[...] / jnp.sum(l_ref[...], axis=-1, keepdims=True)


def _attn_global_kernel(*refs, diff, with_lat, tq, n_lat, lambda_init):
    refs = list(refs)
    q_ref, kc_ref, vc_ref = refs[:3]
    pos = 3
    if with_lat:
        kl_ref, vl_ref = refs[pos:pos + 2]
        pos += 2
    if diff:
        lam_ref, subln_ref = refs[pos:pos + 2]
        pos += 2
    o_ref, qs_ref, m_ref, l_ref, acc_ref = refs[pos:pos + 5]
    pipe_refs = refs[pos + 5:]

    if diff:
        q = q_ref[...]
        lane = lax.broadcasted_iota(jnp.int32, q.shape, 1)
        qs_ref[0:tq, :] = jnp.where(lane < C_QK_DIM, q, jnp.zeros_like(q))
        qs_ref[tq:2 * tq, :] = jnp.where(lane >= C_QK_DIM, q, jnp.zeros_like(q))
        n_stack = 2
    else:
        n_stack = A_HEADS // A_KV_HEADS
        for g in range(n_stack):
            qs_ref[g * tq:(g + 1) * tq, :] = q_ref[:, g * HEAD_DIM:(g + 1) * HEAD_DIM]

    m_ref[...] = jnp.full(m_ref.shape, NEG_INF, F32)
    l_ref[...] = jnp.zeros(l_ref.shape, F32)
    acc_ref[...] = jnp.zeros(acc_ref.shape, F32)

    if with_lat:
        _pipelined_keys(qs_ref, kc_ref, vc_ref, kl_ref, vl_ref, m_ref, l_ref, acc_ref, pipe_refs,
                        n_lat // ATT_TK)
    else:
        _attend_chunk(qs_ref, kc_ref[...], vc_ref[...], m_ref, l_ref, acc_ref)

    o = _attend_result(l_ref, acc_ref)
    if diff:
        lf = lam_ref[...]
        lam = (jnp.exp(jnp.sum(lf[0:1] * lf[1:2], axis=-1, keepdims=True))
               - jnp.exp(jnp.sum(lf[2:3] * lf[3:4], axis=-1, keepdims=True)) + lambda_init)
        d = o[0:tq] - lam * o[tq:2 * tq]
        o_ref[...] = (_rms(d, subln_ref[...]) * (1.0 - lambda_init)).astype(BF16)
    else:
        for g in range(n_stack):
            o_ref[:, g * HEAD_DIM:(g + 1) * HEAD_DIM] = o[g * tq:(g + 1) * tq].astype(BF16)


def _attn_global(q_src, qkv_ctx, qkv_lat, *, diff, tq, n_q, n_ctx, n_lat, batch,
                 lam=None, subln=None, lambda_init=0.0):
    with_lat = qkv_lat is not None
    q_tiles = n_q // tq
    if diff:
        heads, n_stack, q_w = C_HEADS, 2, HEAD_DIM
        q_col, k_col, v_col = C_Q, C_K, C_V
    else:
        heads, n_stack, q_w = A_KV_HEADS, A_HEADS // A_KV_HEADS, (A_HEADS // A_KV_HEADS) * HEAD_DIM
        q_col, k_col, v_col = A_Q * LANES // q_w, A_K, A_V
    in_specs = [
        pl.BlockSpec((tq, q_w), lambda b, h, t: (b * q_tiles + t, q_col + h)),
        pl.BlockSpec((n_ctx, HEAD_DIM), lambda b, h, t: (b, k_col + h)),
        pl.BlockSpec((n_ctx, HEAD_DIM), lambda b, h, t: (b, v_col + h)),
    ]
    args = [q_src, qkv_ctx, qkv_ctx]
    if with_lat:
        in_specs += [
            pl.BlockSpec((n_lat, HEAD_DIM), lambda b, h, t: (b, k_col + h)),
            pl.BlockSpec((n_lat, HEAD_DIM), lambda b, h, t: (b, v_col + h)),
        ]
        args += [qkv_lat, qkv_lat]
    if diff:
        in_specs += [
            pl.BlockSpec((4, C_QK_DIM), lambda b, h, t: (0, 0)),
            pl.BlockSpec((1, HEAD_DIM), lambda b, h, t: (0, 0)),
        ]
        args += [lam, subln]
    rows = n_stack * tq
    scratch = [
        pltpu.VMEM((rows, HEAD_DIM), BF16),
        pltpu.VMEM((rows, LANES), F32),
        pltpu.VMEM((rows, LANES), F32),
        pltpu.VMEM((rows, HEAD_DIM), F32),
    ]
    if with_lat:
        scratch += ([pltpu.VMEM((rows, ATT_TK), F32)] * 2 + [pltpu.VMEM((rows, ATT_TK), BF16)] * 2
                    + [pltpu.VMEM((rows, LANES), F32)] * 2)
    return pl.pallas_call(
        functools.partial(_attn_global_kernel, diff=diff, with_lat=with_lat, tq=tq, n_lat=n_lat,
                          lambda_init=lambda_init),
        grid=(batch, heads, q_tiles),
        in_specs=in_specs,
        out_specs=pl.BlockSpec((tq, q_w), lambda b, h, t: (b * q_tiles + t, h)),
        out_shape=jax.ShapeDtypeStruct((batch * n_q, BRANCH_WIDTH), BF16),
        scratch_shapes=scratch,
        name="attn_diff" if diff else "attn_global",
        compiler_params=_params(("arbitrary", "arbitrary", "arbitrary")),
    )(*args)


WIN_TQ = 256
WIN_SPAN = WIN_TQ + 2 * WINDOW


def _attn_window_kernel(*refs, with_lat, n_lat):
    if with_lat:
        sink_ref, q_ref, kc_ref, vc_ref, kl_ref, vl_ref, o_ref, qs_ref, sc_ref, pc_ref, linv_ref, \
            sw_ref, pw_ref = refs
    else:
        sink_ref, q_ref, kc_ref, vc_ref, o_ref, qs_ref, sc_ref, pc_ref, linv_ref = refs
    tq = WIN_TQ
    n_stack = B_HEADS // B_KV_HEADS
    h = pl.program_id(1)
    t = pl.program_id(2)
    for g in range(n_stack):
        qs_ref[g * tq:(g + 1) * tq, :] = q_ref[:, g * HEAD_DIM:(g + 1) * HEAD_DIM]

    sc_ref[...] = _dot_nt(qs_ref[...], kc_ref[...])
    if with_lat:
        start = pl.multiple_of(jnp.clip(t * tq - WINDOW, 0, n_lat - WIN_SPAN), WINDOW)
        sw_ref[...] = _dot_nt(qs_ref[...], kl_ref[pl.ds(start, WIN_SPAN), :])
        row = lax.broadcasted_iota(jnp.int32, (tq, WIN_SPAN), 0)
        col = lax.broadcasted_iota(jnp.int32, (tq, WIN_SPAN), 1)
        in_band = jnp.abs(start + col - (t * tq + row)) <= WINDOW

    def lane_blocks(s):
        return [s[:, j * LANES:(j + 1) * LANES] for j in range(s.shape[1] // LANES)]

    for g in range(n_stack):
        rows = slice(g * tq, (g + 1) * tq)
        blocks = lane_blocks(sc_ref[rows, :])
        n_ctx_blocks = len(blocks)
        if with_lat:
            blocks += lane_blocks(jnp.where(in_band, sw_ref[rows, :], NEG_INF))
        sink = jnp.full((tq, LANES), sink_ref[h * n_stack + g] * LOG2_E, F32)
        m = jnp.maximum(sink, jnp.max(functools.reduce(jnp.maximum, blocks), axis=-1, keepdims=True))
        p = [jnp.exp2(b - m) for b in blocks]
        denom = jnp.sum(functools.reduce(jnp.add, p), axis=-1, keepdims=True) + jnp.exp2(sink - m)
        linv_ref[rows, :] = 1.0 / denom
        pc_ref[rows, :] = jnp.concatenate(p[:n_ctx_blocks], axis=1).astype(BF16)
        if with_lat:
            pw_ref[rows, :] = jnp.concatenate(p[n_ctx_blocks:], axis=1).astype(BF16)

    o = _dot(pc_ref[...], vc_ref[...])
    if with_lat:
        o = o + _dot(pw_ref[...], vl_ref[pl.ds(start, WIN_SPAN), :])
    o = o * linv_ref[...]
    for g in range(n_stack):
        o_ref[:, g * HEAD_DIM:(g + 1) * HEAD_DIM] = o[g * tq:(g + 1) * tq].astype(BF16)


def _attn_window(q_src, qkv_ctx, qkv_lat, sink, *, n_q, n_ctx, n_lat, batch):
    with_lat = qkv_lat is not None
    tq = WIN_TQ
    q_tiles = n_q // tq
    n_stack = B_HEADS // B_KV_HEADS
    q_w = n_stack * HEAD_DIM
    q_col = B_Q * LANES // q_w
    in_specs = [
        pl.BlockSpec(memory_space=pltpu.SMEM),
        pl.BlockSpec((tq, q_w), lambda b, h, t: (b * q_tiles + t, q_col + h)),
        pl.BlockSpec((n_ctx, HEAD_DIM), lambda b, h, t: (b, B_K + h)),
        pl.BlockSpec((n_ctx, HEAD_DIM), lambda b, h, t: (b, B_V + h)),
    ]
    args = [sink, q_src, qkv_ctx, qkv_ctx]
    if with_lat:
        in_specs += [
            pl.BlockSpec((n_lat, HEAD_DIM), lambda b, h, t: (b, B_K + h)),
            pl.BlockSpec((n_lat, HEAD_DIM), lambda b, h, t: (b, B_V + h)),
        ]
        args += [qkv_lat, qkv_lat]
    rows = n_stack * tq
    scratch = [
        pltpu.VMEM((rows, HEAD_DIM), BF16),
        pltpu.VMEM((rows, n_ctx), F32),
        pltpu.VMEM((rows, n_ctx), BF16),
        pltpu.VMEM((rows, LANES), F32),
    ]
    if with_lat:
        scratch += [pltpu.VMEM((rows, WIN_SPAN), F32), pltpu.VMEM((rows, WIN_SPAN), BF16)]
    return pl.pallas_call(
        functools.partial(_attn_window_kernel, with_lat=with_lat, n_lat=n_lat),
        grid=(batch, B_KV_HEADS, q_tiles),
        in_specs=in_specs,
        out_specs=pl.BlockSpec((tq, q_w), lambda b, h, t: (b * q_tiles + t, h)),
        out_shape=jax.ShapeDtypeStruct((batch * n_q, BRANCH_WIDTH), BF16),
        scratch_shapes=scratch,
        name="attn_window",
        compiler_params=_params(("arbitrary", "arbitrary", "arbitrary")),
    )(*args)


MERGE_TN = 1024


def _merge_kernel(h_ref, oa_ref, ob_ref, oc_ref, wg_ref, wb_ref, y_ref, acc_ref):
    r = pl.program_id(2)
    gate = jax.nn.sigmoid(_dot(h_ref[...], wg_ref[...]))
    for k, o_ref in enumerate((oa_ref, ob_ref, oc_ref)):
        @pl.when(r == k)
        def _(o_ref=o_ref, k=k):
            part = gate * _dot(o_ref[...], wb_ref[...])
            if k == 0:
                acc_ref[...] = part
            elif k < N_BRANCH - 1:
                acc_ref[...] += part
            else:
                y_ref[...] = (acc_ref[...] + part).astype(BF16)


def _merge(h, o_a, o_b, o_c, w_in, w_branch, *, layer, tm):
    rows = h.shape[0]
    n_col = D_MODEL // MERGE_TN
    gate_col = QKV_WIDTH // MERGE_TN
    o_spec = pl.BlockSpec((tm, BRANCH_WIDTH), lambda i, n, r: (i, 0))
    return pl.pallas_call(
        _merge_kernel,
        grid=(rows // tm, n_col, N_BRANCH),
        in_specs=[
            pl.BlockSpec((tm, D_MODEL), lambda i, n, r: (i, 0)),
            o_spec, o_spec, o_spec,
            pl.BlockSpec((None, D_MODEL, MERGE_TN), lambda i, n, r: (layer, 0, gate_col + r * n_col + n)),
            pl.BlockSpec((None, None, BRANCH_WIDTH, MERGE_TN), lambda i, n, r: (layer, r, 0, n)),
        ],
        out_specs=pl.BlockSpec((tm, MERGE_TN), lambda i, n, r: (i, n)),
        out_shape=jax.ShapeDtypeStruct((rows, D_MODEL), BF16),
        scratch_shapes=[pltpu.VMEM((tm, MERGE_TN), F32)],
        name="branch_merge",
        compiler_params=_params(("arbitrary", "arbitrary", "arbitrary")),
    )(h, o_a, o_b, o_c, w_in, w_branch)


def _out_kernel(y_ref, x_ref, mod_ref, gpost_ref, w_ref, o_ref):
    z = _dot(y_ref[...], w_ref[...])
    o_ref[...] = x_ref[...] + mod_ref[5:6, :] * _rms(z, gpost_ref[...])


def _out_project(y, x, mod, g_post, w_out, *, layer, tm, rows_per_group):
    rows = x.shape[0]
    tiles_per_group = rows_per_group // tm
    return pl.pallas_call(
        _out_kernel,
        grid=(rows // tm,),
        in_specs=[
            pl.BlockSpec((tm, D_MODEL), lambda i: (i, 0)),
            pl.BlockSpec((tm, D_MODEL), lambda i: (i, 0)),
            pl.BlockSpec((None, N_MOD, D_MODEL), lambda i: (i // tiles_per_group, 0, 0)),
            pl.BlockSpec((1, D_MODEL), lambda i: (0, 0)),
            pl.BlockSpec((None, D_MODEL, D_MODEL), lambda i: (layer, 0, 0)),
        ],
        out_specs=pl.BlockSpec((tm, D_MODEL), lambda i: (i, 0)),
        out_shape=jax.ShapeDtypeStruct((rows, D_MODEL), F32),
        name="out_projection",
        compiler_params=_params(("arbitrary",)),
    )(y, x, mod, g_post, w_out)


def kernel(x, c, ctx, c_ctx, w_ada, b_ada, norm_pre, norm_post, w_ff_in, w_ff_out, w_in, qk_norm_a,
           sink_b, lam_c, subln_c, w_branch, w_out):
    batch, n_lat, _ = x.shape
    n_ctx = ctx.shape[1]
    depth = w_ada.shape[0]
    assert (depth, x.shape[2]) == (DEPTH, D_MODEL)

    tm_lat_ffn, tm_lat_proj, tm_lat_merge, tm_lat_out = 512, 1024, 1024, 512
    tm_ctx = batch * n_ctx

    cvec = jnp.concatenate([c, c_ctx[None, :]], axis=0)
    mod_all = _modulation(cvec, w_ada, b_ada)
    tables = _rope_tables(n_lat)

    w_ff_in_h = w_ff_in.astype(BF16)
    w_ff_out_h = w_ff_out.astype(BF16)
    w_in_h = w_in.astype(BF16)
    w_branch_h = w_branch.astype(BF16)
    w_out_h = w_out.astype(BF16)

    x_lat = x.reshape(batch * n_lat, D_MODEL)
    x_ctx = ctx.reshape(batch * n_ctx, D_MODEL)

    for l in range(depth):
        last = l == depth - 1
        lambda_init = 0.8 - 0.6 * math.exp(-0.3 * l)
        mod_lat = mod_all[l, 0:batch].reshape(batch, N_MOD, D_MODEL)
        mod_ctx = mod_all[l, batch:batch + 1].reshape(1, N_MOD, D_MODEL)
        g_pre = [norm_pre[l, s][None, :] for s in range(3)]
        g_post = [norm_post[l, s][None, :] for s in range(3)]
        lat = dict(rows_per_group=n_lat)
        cx = dict(rows_per_group=tm_ctx)

        ffn1 = dict(layer=l, which=0, mod_row=0)
        ffn2 = dict(layer=l, which=1, mod_row=6)
        x_ctx = _ffn_half_step(x_ctx, mod_ctx, g_pre[0], g_post[0], w_ff_in_h, w_ff_out_h,
                               tm=tm_ctx, **ffn1, **cx)
        x_lat = _ffn_half_step(x_lat, mod_lat, g_pre[0], g_post[0], w_ff_in_h, w_ff_out_h,
                               tm=tm_lat_ffn, **ffn1, **lat)

        qkv_ctx, h_ctx = _project(x_ctx, mod_ctx, g_pre[1], w_in_h, qk_norm_a[l], None,
                                  layer=l, tm=tm_ctx, n_pos=n_lat, **cx)
        qkv_lat, h_lat = _project(x_lat, mod_lat, g_pre[1], w_in_h, qk_norm_a[l], tables,
                                  layer=l, tm=tm_lat_proj, n_pos=n_lat, **lat)

        dims = dict(n_ctx=n_ctx, n_lat=n_lat, batch=batch)
        diff_args = dict(lam=lam_c[l], subln=subln_c[l][None, :], lambda_init=lambda_init)
        a_lat = _attn_global(qkv_lat, qkv_ctx, qkv_lat, diff=False, tq=256, n_q=n_lat, **dims)
        b_lat = _attn_window(qkv_lat, qkv_ctx, qkv_lat, sink_b[l], n_q=n_lat, **dims)
        c_lat = _attn_global(qkv_lat, qkv_ctx, qkv_lat, diff=True, tq=512, n_q=n_lat, **dims, **diff_args)
        y_lat = _merge(h_lat, a_lat, b_lat, c_lat, w_in_h, w_branch_h, layer=l, tm=tm_lat_merge)
        x_lat = _out_project(y_lat, x_lat, mod_lat, g_post[1], w_out_h, layer=l, tm=tm_lat_out, **lat)
        x_lat = _ffn_half_step(x_lat, mod_lat, g_pre[2], g_post[2], w_ff_in_h, w_ff_out_h,
                               tm=tm_lat_ffn, **ffn2, **lat)

        if not last:
            a_ctx = _attn_global(qkv_ctx, qkv_ctx, None, diff=False, tq=n_ctx, n_q=n_ctx, **dims)
            b_ctx = _attn_window(qkv_ctx, qkv_ctx, None, sink_b[l], n_q=n_ctx, **dims)
            c_ctx_o = _attn_global(qkv_ctx, qkv_ctx, None, diff=True, tq=n_ctx, n_q=n_ctx, **dims, **diff_args)
            y_ctx = _merge(h_ctx, a_ctx, b_ctx, c_ctx_o, w_in_h, w_branch_h, layer=l, tm=tm_ctx)
            x_ctx = _out_project(y_ctx, x_ctx, mod_ctx, g_post[1], w_out_h, layer=l, tm=tm_ctx, **cx)
            x_ctx = _ffn_half_step(x_ctx, mod_ctx, g_pre[2], g_post[2], w_ff_in_h, w_ff_out_h,
                                   tm=tm_ctx, **ffn2, **cx)

    return x_lat.reshape(batch, n_lat, D_MODEL)
```

```python
import functools
import math

import jax
import jax.numpy as jnp
from jax import lax
from jax.experimental import pallas as pl
from jax.experimental.pallas import tpu as pltpu

D_MODEL = 2048
DEPTH = 4
GRID_W = 64
HEAD_DIM = 128
A_HEADS = 8
A_KV_HEADS = 2
B_HEADS = 8
B_KV_HEADS = 2
C_HEADS = 8
C_QK_DIM = HEAD_DIM // 2
BRANCH_WIDTH = 1024
N_BRANCH = 3
WINDOW = 128
D_FF = 5632
HALF_STEP = 0.5
ROPE_THETA = 10000.0
NORM_EPS = 1e-6
NEG_INF = -1e30
N_MOD = 9
QKV_WIDTH = 6144
LANES = 128
SUBLANES = 8
VMEM_LIMIT = 56 * 1024 * 1024

A_Q, A_K, A_V = 0, 8, 10
B_Q, B_K, B_V = 12, 20, 22
C_Q, C_K, C_V = 24, 32, 40

F32 = jnp.float32
BF16 = jnp.bfloat16

LOG2_E = math.log2(math.e)
Q_GAIN_AB = HEAD_DIM ** -0.5 * LOG2_E
Q_GAIN_C = C_QK_DIM ** -0.5 * LOG2_E


def _params(sem, vmem=VMEM_LIMIT):
    return pltpu.CompilerParams(dimension_semantics=sem, vmem_limit_bytes=vmem)


def _rms(x, g):
    ms = jnp.mean(x * x, axis=-1, keepdims=True)
    return x * lax.rsqrt(ms + NORM_EPS) * g


def _dot(a, b):
    return jnp.dot(a, b, preferred_element_type=F32)


def _dot_nt(a, b):
    return lax.dot_general(a, b, (((1,), (1,)), ((), ())), preferred_element_type=F32)


MOD_TN = 1024


def _mod_kernel(c_ref, w_ref, b_ref, o_ref):
    n_rep = MOD_TN // LANES

    def body(kc, accs):
        r0 = pl.multiple_of(kc * SUBLANES, SUBLANES)
        wk = w_ref[0, pl.ds(r0, SUBLANES), :]
        out = []
        for m in range(3):
            cm = c_ref[m, pl.ds(r0, SUBLANES), :]
            sm = cm * jax.nn.sigmoid(cm)
            out.append(accs[m] + jnp.concatenate([sm] * n_rep, axis=1) * wk)
        return tuple(out)

    zero = jnp.zeros((SUBLANES, MOD_TN), F32)
    accs = lax.fori_loop(0, D_MODEL // SUBLANES, body, (zero, zero, zero), unroll=4)
    bias = b_ref[0]
    for m in range(3):
        o_ref[0, m:m + 1, :] = jnp.sum(accs[m], axis=0, keepdims=True) + bias
    o_ref[0, 3:SUBLANES, :] = jnp.zeros((SUBLANES - 3, MOD_TN), F32)


def _modulation(cvec, w_ada, b_ada):
    width = N_MOD * D_MODEL
    c_b = jnp.broadcast_to(cvec[:, :, None], (3, D_MODEL, LANES))
    return pl.pallas_call(
        _mod_kernel,
        grid=(DEPTH, width // MOD_TN),
        in_specs=[
            pl.BlockSpec((3, D_MODEL, LANES), lambda l, j: (0, 0, 0)),
            pl.BlockSpec((1, D_MODEL, MOD_TN), lambda l, j: (l, 0, j)),
            pl.BlockSpec((1, 1, MOD_TN), lambda l, j: (l, 0, j)),
        ],
        out_specs=pl.BlockSpec((1, SUBLANES, MOD_TN), lambda l, j: (l, 0, j)),
        out_shape=jax.ShapeDtypeStruct((DEPTH, SUBLANES, width), F32),
        name="adaln_modulation",
        compiler_params=_params(("arbitrary", "arbitrary")),
    )(c_b, w_ada, b_ada.reshape(DEPTH, 1, width))


FFN_TF = 512


def _ffn_kernel(x_ref, mod_ref, gpre_ref, gpost_ref, w1g_ref, w1u_ref, w2_ref, o_ref, h_ref, *, mod_row):
    j = pl.program_id(1)

    @pl.when(j == 0)
    def _():
        shift = mod_ref[mod_row:mod_row + 1, :]
        scale = mod_ref[mod_row + 1:mod_row + 2, :]
        h = _rms(x_ref[...], gpre_ref[...]) * (1.0 + scale) + shift
        h_ref[...] = h.astype(BF16)
        o_ref[...] = jnp.zeros(o_ref.shape, F32)

    h = h_ref[...]
    g = _dot(h, w1g_ref[...])
    u = _dot(h, w1u_ref[...])
    a = (g * jax.nn.sigmoid(g)) * u
    o_ref[...] += _dot(a.astype(BF16), w2_ref[...])

    @pl.when(j == pl.num_programs(1) - 1)
    def _():
        gate = mod_ref[mod_row + 2:mod_row + 3, :]
        y = _rms(o_ref[...], gpost_ref[...])
        o_ref[...] = x_ref[...] + HALF_STEP * gate * y


def _ffn_half_step(x, mod, g_pre, g_post, w1, w2, *, layer, which, mod_row, tm, rows_per_group):
    rows = x.shape[0]
    nf = D_FF // FFN_TF
    tiles_per_group = rows_per_group // tm
    return pl.pallas_call(
        functools.partial(_ffn_kernel, mod_row=mod_row),
        grid=(rows // tm, nf),
        in_specs=[
            pl.BlockSpec((tm, D_MODEL), lambda i, j: (i, 0), pipeline_mode=pl.Buffered(1)),
            pl.BlockSpec((None, N_MOD, D_MODEL), lambda i, j: (i // tiles_per_group, 0, 0)),
            pl.BlockSpec((1, D_MODEL), lambda i, j: (0, 0)),
            pl.BlockSpec((1, D_MODEL), lambda i, j: (0, 0)),
            pl.BlockSpec((None, None, D_MODEL, FFN_TF), lambda i, j: (layer, which, 0, j)),
            pl.BlockSpec((None, None, D_MODEL, FFN_TF), lambda i, j: (layer, which, 0, nf + j)),
            pl.BlockSpec((None, None, FFN_TF, D_MODEL), lambda i, j: (layer, which, j, 0)),
        ],
        out_specs=pl.BlockSpec((tm, D_MODEL), lambda i, j: (i, 0)),
        out_shape=jax.ShapeDtypeStruct((rows, D_MODEL), F32),
        scratch_shapes=[pltpu.VMEM((tm, D_MODEL), BF16)],
        name="ffn_half_step",
        compiler_params=_params(("arbitrary", "arbitrary")),
    )(x, mod, g_pre, g_post, w1, w1, w2)


PROJ_TN = 512
PROJ_CHUNKS = PROJ_TN // LANES
PROJ_TILES = QKV_WIDTH // PROJ_TN


def _rope(y, tab_ref, base, quarter):
    cos, sin_up, sin_dn = tab_ref[base], tab_ref[base + 1], tab_ref[base + 2]
    return (y * cos + pltpu.roll(y, LANES - quarter, 1) * sin_up + pltpu.roll(y, quarter, 1) * sin_dn)


def _head_chunk_kind(u):
    rope_ab, rope_c = (0, HEAD_DIM // 4), (3, C_QK_DIM // 4)
    if u < A_K:
        return 0, rope_ab, Q_GAIN_AB
    if u < A_V:
        return 1, rope_ab, None
    if B_Q <= u < B_K:
        return None, rope_ab, Q_GAIN_AB
    if B_K <= u < B_V:
        return None, rope_ab, None
    if C_Q <= u < C_K:
        return None, rope_c, Q_GAIN_C
    if C_K <= u < C_V:
        return None, rope_c, None
    return None, None, None


def _proj_kernel(*refs, use_rope):
    if use_rope:
        x_ref, mod_ref, gpre_ref, w_ref, qkn_ref, tab_ref, o_ref, ho_ref, h_ref, y0_ref, y1_ref = refs
    else:
        x_ref, mod_ref, gpre_ref, w_ref, qkn_ref, o_ref, ho_ref, h_ref, y0_ref, y1_ref = refs
        tab_ref = None
    y_refs = (y0_ref, y1_ref)
    j = pl.program_id(1)

    def prologue():
        shift = mod_ref[3:4, :]
        scale = mod_ref[4:5, :]
        h = (_rms(x_ref[...], gpre_ref[...]) * (1.0 + scale) + shift).astype(BF16)
        h_ref[...] = h
        ho_ref[...] = h

    def start(t):
        y_refs[t % 2][...] = _dot(h_ref[...], w_ref[...])

    def finish(t):
        y_ref = y_refs[t % 2]
        for c in range(PROJ_CHUNKS):
            norm_row, rope, gain = _head_chunk_kind(t * PROJ_CHUNKS + c)
            v = y_ref[:, c * LANES:(c + 1) * LANES]
            if norm_row is not None:
                v = _rms(v, qkn_ref[norm_row:norm_row + 1, :])
            if rope is not None and use_rope:
                v = _rope(v, tab_ref, *rope)
            if gain is not None:
                v = v * gain
            o_ref[:, c * LANES:(c + 1) * LANES] = v.astype(BF16)

    for step in range(PROJ_TILES + 1):
        @pl.when(j == step)
        def _(step=step):
            if step == 0:
                prologue()
            else:
                finish(step - 1)
            if step < PROJ_TILES:
                start(step)


def _project(x, mod, g_pre, w_in, qk_norm, tables, *, layer, tm, rows_per_group, n_pos):
    rows = x.shape[0]
    use_rope = tables is not None
    tiles_per_group = rows_per_group // tm
    last = PROJ_TILES - 1
    in_specs = [
        pl.BlockSpec((tm, D_MODEL), lambda i, j: (i, 0), pipeline_mode=pl.Buffered(1)),
        pl.BlockSpec((None, N_MOD, D_MODEL), lambda i, j: (i // tiles_per_group, 0, 0)),
        pl.BlockSpec((1, D_MODEL), lambda i, j: (0, 0)),
        pl.BlockSpec((None, D_MODEL, PROJ_TN), lambda i, j: (layer, 0, jnp.minimum(j, last))),
        pl.BlockSpec((2, HEAD_DIM), lambda i, j: (0, 0)),
    ]
    args = [x, mod, g_pre, w_in, qk_norm]
    if use_rope:
        pos_tiles = n_pos // tm
        in_specs.append(pl.BlockSpec((6, tm, LANES), lambda i, j: (0, i % pos_tiles, 0)))
        args.append(tables)
    return pl.pallas_call(
        functools.partial(_proj_kernel, use_rope=use_rope),
        grid=(rows // tm, PROJ_TILES + 1),
        in_specs=in_specs,
        out_specs=[
            pl.BlockSpec((tm, PROJ_TN), lambda i, j: (i, jnp.maximum(j - 1, 0))),
            pl.BlockSpec((tm, D_MODEL), lambda i, j: (i, 0)),
        ],
        out_shape=[
            jax.ShapeDtypeStruct((rows, QKV_WIDTH), BF16),
            jax.ShapeDtypeStruct((rows, D_MODEL), BF16),
        ],
        scratch_shapes=[pltpu.VMEM((tm, D_MODEL), BF16),
                        pltpu.VMEM((tm, PROJ_TN), F32), pltpu.VMEM((tm, PROJ_TN), F32)],
        name="in_projection",
        compiler_params=_params(("arbitrary", "arbitrary")),
    )(*args)


def _rope_tables(n_tok):
    rows = n_tok // GRID_W
    row = jnp.repeat(jnp.arange(rows, dtype=jnp.int32), GRID_W).astype(F32)
    col = jnp.tile(jnp.arange(GRID_W, dtype=jnp.int32), rows).astype(F32)
    out = []
    for rot_dim in (HEAD_DIM, C_QK_DIM):
        d_ax = rot_dim // 2
        inv = ROPE_THETA ** (-jnp.arange(0, d_ax, 2, dtype=F32) / d_ax)
        ang_r = row[:, None] * inv[None, :]
        ang_c = col[:, None] * inv[None, :]
        ang = jnp.concatenate([ang_r, ang_r, ang_c, ang_c], axis=-1)
        reps = LANES // rot_dim
        cos = jnp.tile(jnp.cos(ang), (1, reps))
        sin = jnp.tile(jnp.sin(ang), (1, reps))
        quarter = (jnp.arange(LANES) % rot_dim) // (rot_dim // 4)
        even = (quarter % 2 == 0)[None, :]
        out += [cos, jnp.where(even, -sin, 0.0), jnp.where(even, 0.0, sin)]
    return jnp.stack(out, axis=0)


ATT_TK = 512
ATT_RB = 128


def _softmax_block(s, v, rows, m_ref, l_ref, acc_ref):
    blocks = [s[:, j * LANES:(j + 1) * LANES] for j in range(s.shape[1] // LANES)]
    m_prev = m_ref[rows, :]
    m_new = jnp.maximum(m_prev, jnp.max(functools.reduce(jnp.maximum, blocks), axis=-1, keepdims=True))
    alpha = jnp.exp2(m_prev - m_new)
    p = [jnp.exp2(b - m_new) for b in blocks]
    l_ref[rows, :] = alpha * l_ref[rows, :] + functools.reduce(jnp.add, p)
    pv = _dot(jnp.concatenate(p, axis=1).astype(BF16), v)
    acc_ref[rows, :] = alpha * acc_ref[rows, :] + pv
    m_ref[rows, :] = m_new


def _attend_chunk(qs_ref, k, v, m_ref, l_ref, acc_ref, mask_fn=None):
    for r in range(qs_ref.shape[0] // ATT_RB):
        rows = slice(r * ATT_RB, (r + 1) * ATT_RB)
        s = _dot_nt(qs_ref[rows, :], k)
        if mask_fn is not None:
            s = mask_fn(s, r * ATT_RB)
        _softmax_block(s, v, rows, m_ref, l_ref, acc_ref)


def _pipelined_keys(qs_ref, kc_ref, vc_ref, k_ref, v_ref, m_ref, l_ref, acc_ref, pipe_refs, n_chunks):
    s_refs, p_refs, a_refs = pipe_refs[0:2], pipe_refs[2:4], pipe_refs[4:6]
    n_rows = qs_ref.shape[0]
    n_ctx = kc_ref.shape[0]
    assert n_chunks % 2 == 0 and n_ctx <= ATT_TK

    def chunk(ref, c):
        return ref[pl.ds(pl.multiple_of(c * ATT_TK, ATT_TK), ATT_TK), :]

    def scores(k, buf):
        s_refs[buf][:, 0:k.shape[0]] = _dot_nt(qs_ref[...], k)

    def softmax(buf, width=ATT_TK):
        for r in range(n_rows // ATT_RB):
            rows = slice(r * ATT_RB, (r + 1) * ATT_RB)
            s = s_refs[buf][rows, 0:width]
            blocks = [s[:, j * LANES:(j + 1) * LANES] for j in range(width // LANES)]
            m_prev = m_ref[rows, :]
            m_new = jnp.maximum(
                m_prev, jnp.max(functools.reduce(jnp.maximum, blocks), axis=-1, keepdims=True))
            alpha = jnp.exp2(m_prev - m_new)
            p = [jnp.exp2(b - m_new) for b in blocks]
            l_ref[rows, :] = alpha * l_ref[rows, :] + functools.reduce(jnp.add, p)
            m_ref[rows, :] = m_new
            a_refs[buf][rows, :] = alpha
            p_refs[buf][rows, 0:width] = jnp.concatenate(p, axis=1).astype(BF16)

    def values(v, buf):
        pv = _dot(p_refs[buf][:, 0:v.shape[0]], v)
        acc_ref[...] = a_refs[buf][...] * acc_ref[...] + pv

    scores(kc_ref[...], 0)
    softmax(0, n_ctx)
    scores(chunk(k_ref, 0), 1)
    values(vc_ref[...], 0)
    softmax(1)
    scores(chunk(k_ref, 1), 0)

    def body(i, carry):
        c = 2 * i + 1
        values(chunk(v_ref, c - 1), 1)
        softmax(0)
        scores(chunk(k_ref, c + 1), 1)
        values(chunk(v_ref, c), 0)
        softmax(1)
        scores(chunk(k_ref, c + 2), 0)
        return carry

    lax.fori_loop(0, n_chunks // 2 - 1, body, 0, unroll=True)
    values(chunk(v_ref, n_chunks - 2), 1)
    softmax(0)
    values(chunk(v_ref, n_chunks - 1), 0)


def _attend_result(l_ref, acc_ref):
    return acc_ref[...] / jnp.sum(l_ref[...], axis=-1, keepdims=True)


def _attn_global_kernel(*refs, diff, with_lat, tq, n_lat, lambda_init):
    refs = list(refs)
    q_ref, kc_ref, vc_ref = refs[:3]
    pos = 3
    if with_lat:
        kl_ref, vl_ref = refs[pos:pos + 2]
        pos += 2
    if diff:
        lam_ref, subln_ref = refs[pos:pos + 2]
        pos += 2
    o_ref, qs_ref, m_ref, l_ref, acc_ref = refs[pos:pos + 5]
    pipe_refs = refs[pos + 5:]

    if diff:
        q = q_ref[...]
        lane = lax.broadcasted_iota(jnp.int32, q.shape, 1)
        qs_ref[0:tq, :] = jnp.where(lane < C_QK_DIM, q, jnp.zeros_like(q))
        qs_ref[tq:2 * tq, :] = jnp.where(lane >= C_QK_DIM, q, jnp.zeros_like(q))
        n_stack = 2
    else:
        n_stack = A_HEADS // A_KV_HEADS
        for g in range(n_stack):
            qs_ref[g * tq:(g + 1) * tq, :] = q_ref[:, g * HEAD_DIM:(g + 1) * HEAD_DIM]

    m_ref[...] = jnp.full(m_ref.shape, NEG_INF, F32)
    l_ref[...] = jnp.zeros(l_ref.shape, F32)
    acc_ref[...] = jnp.zeros(acc_ref.shape, F32)

    if with_lat:
        _pipelined_keys(qs_ref, kc_ref, vc_ref, kl_ref, vl_ref, m_ref, l_ref, acc_ref, pipe_refs,
                        n_lat // ATT_TK)
    else:
        _attend_chunk(qs_ref, kc_ref[...], vc_ref[...], m_ref, l_ref, acc_ref)

    o = _attend_result(l_ref, acc_ref)
    if diff:
        lf = lam_ref[...]
        lam = (jnp.exp(jnp.sum(lf[0:1] * lf[1:2], axis=-1, keepdims=True))
               - jnp.exp(jnp.sum(lf[2:3] * lf[3:4], axis=-1, keepdims=True)) + lambda_init)
        d = o[0:tq] - lam * o[tq:2 * tq]
        o_ref[...] = (_rms(d, subln_ref[...]) * (1.0 - lambda_init)).astype(BF16)
    else:
        for g in range(n_stack):
            o_ref[:, g * HEAD_DIM:(g + 1) * HEAD_DIM] = o[g * tq:(g + 1) * tq].astype(BF16)


def _attn_global(q_src, qkv_ctx, qkv_lat, *, diff, tq, n_q, n_ctx, n_lat, batch,
                 lam=None, subln=None, lambda_init=0.0):
    with_lat = qkv_lat is not None
    q_tiles = n_q // tq
    if diff:
        heads, n_stack, q_w = C_HEADS, 2, HEAD_DIM
        q_col, k_col, v_col = C_Q, C_K, C_V
    else:
        heads, n_stack, q_w = A_KV_HEADS, A_HEADS // A_KV_HEADS, (A_HEADS // A_KV_HEADS) * HEAD_DIM
        q_col, k_col, v_col = A_Q * LANES // q_w, A_K, A_V
    in_specs = [
        pl.BlockSpec((tq, q_w), lambda b, h, t: (b * q_tiles + t, q_col + h)),
        pl.BlockSpec((n_ctx, HEAD_DIM), lambda b, h, t: (b, k_col + h)),
        pl.BlockSpec((n_ctx, HEAD_DIM), lambda b, h, t: (b, v_col + h)),
    ]
    args = [q_src, qkv_ctx, qkv_ctx]
    if with_lat:
        in_specs += [
            pl.BlockSpec((n_lat, HEAD_DIM), lambda b, h, t: (b, k_col + h)),
            pl.BlockSpec((n_lat, HEAD_DIM), lambda b, h, t: (b, v_col + h)),
        ]
        args += [qkv_lat, qkv_lat]
    if diff:
        in_specs += [
            pl.BlockSpec((4, C_QK_DIM), lambda b, h, t: (0, 0)),
            pl.BlockSpec((1, HEAD_DIM), lambda b, h, t: (0, 0)),
        ]
        args += [lam, subln]
    rows = n_stack * tq
    scratch = [
        pltpu.VMEM((rows, HEAD_DIM), BF16),
        pltpu.VMEM((rows, LANES), F32),
        pltpu.VMEM((rows, LANES), F32),
        pltpu.VMEM((rows, HEAD_DIM), F32),
    ]
    if with_lat:
        scratch += ([pltpu.VMEM((rows, ATT_TK), F32)] * 2 + [pltpu.VMEM((rows, ATT_TK), BF16)] * 2
                    + [pltpu.VMEM((rows, LANES), F32)] * 2)
    return pl.pallas_call(
        functools.partial(_attn_global_kernel, diff=diff, with_lat=with_lat, tq=tq, n_lat=n_lat,
                          lambda_init=lambda_init),
        grid=(batch, heads, q_tiles),
        in_specs=in_specs,
        out_specs=pl.BlockSpec((tq, q_w), lambda b, h, t: (b * q_tiles + t, h)),
        out_shape=jax.ShapeDtypeStruct((batch * n_q, BRANCH_WIDTH), BF16),
        scratch_shapes=scratch,
        name="attn_diff" if diff else "attn_global",
        compiler_params=_params(("arbitrary", "arbitrary", "arbitrary")),
    )(*args)


WIN_TQ = 256
WIN_SPAN = WIN_TQ + 2 * WINDOW


def _attn_window_kernel(*refs, with_lat, n_lat):
    if with_lat:
        sink_ref, q_ref, kc_ref, vc_ref, kl_ref, vl_ref, o_ref, qs_ref, sc_ref, pc_ref, linv_ref, \
            sw_ref, pw_ref = refs
    else:
        sink_ref, q_ref, kc_ref, vc_ref, o_ref, qs_ref, sc_ref, pc_ref, linv_ref = refs
    tq = WIN_TQ
    n_stack = B_HEADS // B_KV_HEADS
    h = pl.program_id(1)
    t = pl.program_id(2)
    for g in range(n_stack):
        qs_ref[g * tq:(g + 1) * tq, :] = q_ref[:, g * HEAD_DIM:(g + 1) * HEAD_DIM]

    sc_ref[...] = _dot_nt(qs_ref[...], kc_ref[...])
    if with_lat:
        start = pl.multiple_of(jnp.clip(t * tq - WINDOW, 0, n_lat - WIN_SPAN), WINDOW)
        sw_ref[...] = _dot_nt(qs_ref[...], kl_ref[pl.ds(start, WIN_SPAN), :])
        row = lax.broadcasted_iota(jnp.int32, (tq, WIN_SPAN), 0)
        col = lax.broadcasted_iota(jnp.int32, (tq, WIN_SPAN), 1)
        in_band = jnp.abs(start + col - (t * tq + row)) <= WINDOW

    def lane_blocks(s):
        return [s[:, j * LANES:(j + 1) * LANES] for j in range(s.shape[1] // LANES)]

    for g in range(n_stack):
        rows = slice(g * tq, (g + 1) * tq)
        blocks = lane_blocks(sc_ref[rows, :])
        n_ctx_blocks = len(blocks)
        if with_lat:
            blocks += lane_blocks(jnp.where(in_band, sw_ref[rows, :], NEG_INF))
        sink = jnp.full((tq, LANES), sink_ref[h * n_stack + g] * LOG2_E, F32)
        m = jnp.maximum(sink, jnp.max(functools.reduce(jnp.maximum, blocks), axis=-1, keepdims=True))
        p = [jnp.exp2(b - m) for b in blocks]
        denom = jnp.sum(functools.reduce(jnp.add, p), axis=-1, keepdims=True) + jnp.exp2(sink - m)
        linv_ref[rows, :] = 1.0 / denom
        pc_ref[rows, :] = jnp.concatenate(p[:n_ctx_blocks], axis=1).astype(BF16)
        if with_lat:
            pw_ref[rows, :] = jnp.concatenate(p[n_ctx_blocks:], axis=1).astype(BF16)

    o = _dot(pc_ref[...], vc_ref[...])
    if with_lat:
        o = o + _dot(pw_ref[...], vl_ref[pl.ds(start, WIN_SPAN), :])
    o = o * linv_ref[...]
    for g in range(n_stack):
        o_ref[:, g * HEAD_DIM:(g + 1) * HEAD_DIM] = o[g * tq:(g + 1) * tq].astype(BF16)


def _attn_window(q_src, qkv_ctx, qkv_lat, sink, *, n_q, n_ctx, n_lat, batch):
    with_lat = qkv_lat is not None
    tq = WIN_TQ
    q_tiles = n_q // tq
    n_stack = B_HEADS // B_KV_HEADS
    q_w = n_stack * HEAD_DIM
    q_col = B_Q * LANES // q_w
    in_specs = [
        pl.BlockSpec(memory_space=pltpu.SMEM),
        pl.BlockSpec((tq, q_w), lambda b, h, t: (b * q_tiles + t, q_col + h)),
        pl.BlockSpec((n_ctx, HEAD_DIM), lambda b, h, t: (b, B_K + h)),
        pl.BlockSpec((n_ctx, HEAD_DIM), lambda b, h, t: (b, B_V + h)),
    ]
    args = [sink, q_src, qkv_ctx, qkv_ctx]
    if with_lat:
        in_specs += [
            pl.BlockSpec((n_lat, HEAD_DIM), lambda b, h, t: (b, B_K + h)),
            pl.BlockSpec((n_lat, HEAD_DIM), lambda b, h, t: (b, B_V + h)),
        ]
        args += [qkv_lat, qkv_lat]
    rows = n_stack * tq
    scratch = [
        pltpu.VMEM((rows, HEAD_DIM), BF16),
        pltpu.VMEM((rows, n_ctx), F32),
        pltpu.VMEM((rows, n_ctx), BF16),
        pltpu.VMEM((rows, LANES), F32),
    ]
    if with_lat:
        scratch += [pltpu.VMEM((rows, WIN_SPAN), F32), pltpu.VMEM((rows, WIN_SPAN), BF16)]
    return pl.pallas_call(
        functools.partial(_attn_window_kernel, with_lat=with_lat, n_lat=n_lat),
        grid=(batch, B_KV_HEADS, q_tiles),
        in_specs=in_specs,
        out_specs=pl.BlockSpec((tq, q_w), lambda b, h, t: (b * q_tiles + t, h)),
        out_shape=jax.ShapeDtypeStruct((batch * n_q, BRANCH_WIDTH), BF16),
        scratch_shapes=scratch,
        name="attn_window",
        compiler_params=_params(("arbitrary", "arbitrary", "arbitrary")),
    )(*args)


MERGE_TN = 1024


def _merge_kernel(h_ref, oa_ref, ob_ref, oc_ref, wg_ref, wb_ref, y_ref, acc_ref):
    r = pl.program_id(2)
    gate = jax.nn.sigmoid(_dot(h_ref[...], wg_ref[...]))
    for k, o_ref in enumerate((oa_ref, ob_ref, oc_ref)):
        @pl.when(r == k)
        def _(o_ref=o_ref, k=k):
            part = gate * _dot(o_ref[...], wb_ref[...])
            if k == 0:
                acc_ref[...] = part
            elif k < N_BRANCH - 1:
                acc_ref[...] += part
            else:
                y_ref[...] = (acc_ref[...] + part).astype(BF16)


def _merge(h, o_a, o_b, o_c, w_in, w_branch, *, layer, tm):
    rows = h.shape[0]
    n_col = D_MODEL // MERGE_TN
    gate_col = QKV_WIDTH // MERGE_TN
    o_spec = pl.BlockSpec((tm, BRANCH_WIDTH), lambda i, n, r: (i, 0))
    return pl.pallas_call(
        _merge_kernel,
        grid=(rows // tm, n_col, N_BRANCH),
        in_specs=[
            pl.BlockSpec((tm, D_MODEL), lambda i, n, r: (i, 0)),
            o_spec, o_spec, o_spec,
            pl.BlockSpec((None, D_MODEL, MERGE_TN), lambda i, n, r: (layer, 0, gate_col + r * n_col + n)),
            pl.BlockSpec((None, None, BRANCH_WIDTH, MERGE_TN), lambda i, n, r: (layer, r, 0, n)),
        ],
        out_specs=pl.BlockSpec((tm, MERGE_TN), lambda i, n, r: (i, n)),
        out_shape=jax.ShapeDtypeStruct((rows, D_MODEL), BF16),
        scratch_shapes=[pltpu.VMEM((tm, MERGE_TN), F32)],
        name="branch_merge",
        compiler_params=_params(("arbitrary", "arbitrary", "arbitrary")),
    )(h, o_a, o_b, o_c, w_in, w_branch)


def _out_kernel(y_ref, x_ref, mod_ref, gpost_ref, w_ref, o_ref):
    z = _dot(y_ref[...], w_ref[...])
    o_ref[...] = x_ref[...] + mod_ref[5:6, :] * _rms(z, gpost_ref[...])


def _out_project(y, x, mod, g_post, w_out, *, layer, tm, rows_per_group):
    rows = x.shape[0]
    tiles_per_group = rows_per_group // tm
    return pl.pallas_call(
        _out_kernel,
        grid=(rows // tm,),
        in_specs=[
            pl.BlockSpec((tm, D_MODEL), lambda i: (i, 0)),
            pl.BlockSpec((tm, D_MODEL), lambda i: (i, 0)),
            pl.BlockSpec((None, N_MOD, D_MODEL), lambda i: (i // tiles_per_group, 0, 0)),
            pl.BlockSpec((1, D_MODEL), lambda i: (0, 0)),
            pl.BlockSpec((None, D_MODEL, D_MODEL), lambda i: (layer, 0, 0)),
        ],
        out_specs=pl.BlockSpec((tm, D_MODEL), lambda i: (i, 0)),
        out_shape=jax.ShapeDtypeStruct((rows, D_MODEL), F32),
        name="out_projection",
        compiler_params=_params(("arbitrary",)),
    )(y, x, mod, g_post, w_out)


def kernel(x, c, ctx, c_ctx, w_ada, b_ada, norm_pre, norm_post, w_ff_in, w_ff_out, w_in, qk_norm_a,
           sink_b, lam_c, subln_c, w_branch, w_out):
    batch, n_lat, _ = x.shape
    n_ctx = ctx.shape[1]
    depth = w_ada.shape[0]
    assert (depth, x.shape[2]) == (DEPTH, D_MODEL)

    tm_lat_ffn, tm_lat_proj, tm_lat_merge, tm_lat_out = 1024, 1024, 1024, 512
    tm_ctx = batch * n_ctx

    cvec = jnp.concatenate([c, c_ctx[None, :]], axis=0)
    mod_all = _modulation(cvec, w_ada, b_ada)
    tables = _rope_tables(n_lat)

    w_ff_in_h = w_ff_in.astype(BF16)
    w_ff_out_h = w_ff_out.astype(BF16)
    w_in_h = w_in.astype(BF16)
    w_branch_h = w_branch.astype(BF16)
    w_out_h = w_out.astype(BF16)

    x_lat = x.reshape(batch * n_lat, D_MODEL)
    x_ctx = ctx.reshape(batch * n_ctx, D_MODEL)

    for l in range(depth):
        last = l == depth - 1
        lambda_init = 0.8 - 0.6 * math.exp(-0.3 * l)
        mod_lat = mod_all[l, 0:batch].reshape(batch, N_MOD, D_MODEL)
        mod_ctx = mod_all[l, batch:batch + 1].reshape(1, N_MOD, D_MODEL)
        g_pre = [norm_pre[l, s][None, :] for s in range(3)]
        g_post = [norm_post[l, s][None, :] for s in range(3)]
        lat = dict(rows_per_group=n_lat)
        cx = dict(rows_per_group=tm_ctx)

        ffn1 = dict(layer=l, which=0, mod_row=0)
        ffn2 = dict(layer=l, which=1, mod_row=6)
        x_ctx = _ffn_half_step(x_ctx, mod_ctx, g_pre[0], g_post[0], w_ff_in_h, w_ff_out_h,
                               tm=tm_ctx, **ffn1, **cx)
        x_lat = _ffn_half_step(x_lat, mod_lat, g_pre[0], g_post[0], w_ff_in_h, w_ff_out_h,
                               tm=tm_lat_ffn, **ffn1, **lat)

        qkv_ctx, h_ctx = _project(x_ctx, mod_ctx, g_pre[1], w_in_h, qk_norm_a[l], None,
                                  layer=l, tm=tm_ctx, n_pos=n_lat, **cx)
        qkv_lat, h_lat = _project(x_lat, mod_lat, g_pre[1], w_in_h, qk_norm_a[l], tables,
                                  layer=l, tm=tm_lat_proj, n_pos=n_lat, **lat)

        dims = dict(n_ctx=n_ctx, n_lat=n_lat, batch=batch)
        diff_args = dict(lam=lam_c[l], subln=subln_c[l][None, :], lambda_init=lambda_init)
        a_lat = _attn_global(qkv_lat, qkv_ctx, qkv_lat, diff=False, tq=256, n_q=n_lat, **dims)
        b_lat = _attn_window(qkv_lat, qkv_ctx, qkv_lat, sink_b[l], n_q=n_lat, **dims)
        c_lat = _attn_global(qkv_lat, qkv_ctx, qkv_lat, diff=True, tq=512, n_q=n_lat, **dims, **diff_args)
        y_lat = _merge(h_lat, a_lat, b_lat, c_lat, w_in_h, w_branch_h, layer=l, tm=tm_lat_merge)
        x_lat = _out_project(y_lat, x_lat, mod_lat, g_post[1], w_out_h, layer=l, tm=tm_lat_out, **lat)
        x_lat = _ffn_half_step(x_lat, mod_lat, g_pre[2], g_post[2], w_ff_in_h, w_ff_out_h,
                               tm=tm_lat_ffn, **ffn2, **lat)

        if not last:
            a_ctx = _attn_global(qkv_ctx, qkv_ctx, None, diff=False, tq=n_ctx, n_q=n_ctx, **dims)
            b_ctx = _attn_window(qkv_ctx, qkv_ctx, None, sink_b[l], n_q=n_ctx, **dims)
            c_ctx_o = _attn_global(qkv_ctx, qkv_ctx, None, diff=True, tq=n_ctx, n_q=n_ctx, **dims, **diff_args)
            y_ctx = _merge(h_ctx, a_ctx, b_ctx, c_ctx_o, w_in_h, w_branch_h, layer=l, tm=tm_ctx)
            x_ctx = _out_project(y_ctx, x_ctx, mod_ctx, g_post[1], w_out_h, layer=l, tm=tm_ctx, **cx)
            x_ctx = _ffn_half_step(x_ctx, mod_ctx, g_pre[2], g_post[2], w_ff_in_h, w_ff_out_h,
                                   tm=tm_ctx, **ffn2, **cx)

    return x_lat.reshape(batch, n_lat, D_MODEL)
```

```python
import functools
import math

import jax
import jax.numpy as jnp
from jax import lax
from jax.experimental import pallas as pl
from jax.experimental.pallas import tpu as pltpu

D_MODEL = 2048
DEPTH = 4
GRID_W = 64
HEAD_DIM = 128
A_HEADS = 8
A_KV_HEADS = 2
B_HEADS = 8
B_KV_HEADS = 2
C_HEADS = 8
C_QK_DIM = HEAD_DIM // 2
BRANCH_WIDTH = 1024
N_BRANCH = 3
WINDOW = 128
D_FF = 5632
HALF_STEP = 0.5
ROPE_THETA = 10000.0
NORM_EPS = 1e-6
NEG_INF = -1e30
N_MOD = 9
QKV_WIDTH = 6144
LANES = 128
SUBLANES = 8
VMEM_LIMIT = 56 * 1024 * 1024

A_Q, A_K, A_V = 0, 8, 10
B_Q, B_K, B_V = 12, 20, 22
C_Q, C_K, C_V = 24, 32, 40

F32 = jnp.float32
BF16 = jnp.bfloat16

LOG2_E = math.log2(math.e)
Q_GAIN_AB = HEAD_DIM ** -0.5 * LOG2_E
Q_GAIN_C = C_QK_DIM ** -0.5 * LOG2_E


def _params(sem, vmem=VMEM_LIMIT):
    return pltpu.CompilerParams(dimension_semantics=sem, vmem_limit_bytes=vmem)


def _rms(x, g):
    ms = jnp.mean(x * x, axis=-1, keepdims=True)
    return x * lax.rsqrt(ms + NORM_EPS) * g


def _dot(a, b):
    return jnp.dot(a, b, preferred_element_type=F32)


def _dot_nt(a, b):
    return lax.dot_general(a, b, (((1,), (1,)), ((), ())), preferred_element_type=F32)


MOD_TN = 1024


def _mod_kernel(c_ref, w_ref, b_ref, o_ref):
    n_rep = MOD_TN // LANES

    def body(kc, accs):
        r0 = pl.multiple_of(kc * SUBLANES, SUBLANES)
        wk = w_ref[0, pl.ds(r0, SUBLANES), :]
        out = []
        for m in range(3):
            cm = c_ref[m, pl.ds(r0, SUBLANES), :]
            sm = cm * jax.nn.sigmoid(cm)
            out.append(accs[m] + jnp.concatenate([sm] * n_rep, axis=1) * wk)
        return tuple(out)

    zero = jnp.zeros((SUBLANES, MOD_TN), F32)
    accs = lax.fori_loop(0, D_MODEL // SUBLANES, body, (zero, zero, zero), unroll=4)
    bias = b_ref[0]
    for m in range(3):
        o_ref[0, m:m + 1, :] = jnp.sum(accs[m], axis=0, keepdims=True) + bias
    o_ref[0, 3:SUBLANES, :] = jnp.zeros((SUBLANES - 3, MOD_TN), F32)


def _modulation(cvec, w_ada, b_ada):
    width = N_MOD * D_MODEL
    c_b = jnp.broadcast_to(cvec[:, :, None], (3, D_MODEL, LANES))
    return pl.pallas_call(
        _mod_kernel,
        grid=(DEPTH, width // MOD_TN),
        in_specs=[
            pl.BlockSpec((3, D_MODEL, LANES), lambda l, j: (0, 0, 0)),
            pl.BlockSpec((1, D_MODEL, MOD_TN), lambda l, j: (l, 0, j)),
            pl.BlockSpec((1, 1, MOD_TN), lambda l, j: (l, 0, j)),
        ],
        out_specs=pl.BlockSpec((1, SUBLANES, MOD_TN), lambda l, j: (l, 0, j)),
        out_shape=jax.ShapeDtypeStruct((DEPTH, SUBLANES, width), F32),
        name="adaln_modulation",
        compiler_params=_params(("arbitrary", "arbitrary")),
    )(c_b, w_ada, b_ada.reshape(DEPTH, 1, width))


FFN_TF = 512


def _ffn_kernel(x_ref, mod_ref, gpre_ref, gpost_ref, w1g_ref, w1u_ref, w2_ref, o_ref, h_ref, *, mod_row):
    j = pl.program_id(1)

    @pl.when(j == 0)
    def _():
        shift = mod_ref[mod_row:mod_row + 1, :]
        scale = mod_ref[mod_row + 1:mod_row + 2, :]
        h = _rms(x_ref[...], gpre_ref[...]) * (1.0 + scale) + shift
        h_ref[...] = h.astype(BF16)
        o_ref[...] = jnp.zeros(o_ref.shape, F32)

    h = h_ref[...]
    g = _dot(h, w1g_ref[...])
    u = _dot(h, w1u_ref[...])
    a = (g * jax.nn.sigmoid(g)) * u
    o_ref[...] += _dot(a.astype(BF16), w2_ref[...])

    @pl.when(j == pl.num_programs(1) - 1)
    def _():
        gate = mod_ref[mod_row + 2:mod_row + 3, :]
        y = _rms(o_ref[...], gpost_ref[...])
        o_ref[...] = x_ref[...] + HALF_STEP * gate * y


def _ffn_half_step(x, mod, g_pre, g_post, w1, w2, *, layer, which, mod_row, tm, rows_per_group):
    rows = x.shape[0]
    nf = D_FF // FFN_TF
    tiles_per_group = rows_per_group // tm
    return pl.pallas_call(
        functools.partial(_ffn_kernel, mod_row=mod_row),
        grid=(rows // tm, nf),
        in_specs=[
            pl.BlockSpec((tm, D_MODEL), lambda i, j: (i, 0)),
            pl.BlockSpec((None, N_MOD, D_MODEL), lambda i, j: (i // tiles_per_group, 0, 0)),
            pl.BlockSpec((1, D_MODEL), lambda i, j: (0, 0)),
            pl.BlockSpec((1, D_MODEL), lambda i, j: (0, 0)),
            pl.BlockSpec((None, None, D_MODEL, FFN_TF), lambda i, j: (layer, which, 0, j)),
            pl.BlockSpec((None, None, D_MODEL, FFN_TF), lambda i, j: (layer, which, 0, nf + j)),
            pl.BlockSpec((None, None, FFN_TF, D_MODEL), lambda i, j: (layer, which, j, 0)),
        ],
        out_specs=pl.BlockSpec((tm, D_MODEL), lambda i, j: (i, 0)),
        out_shape=jax.ShapeDtypeStruct((rows, D_MODEL), F32),
        scratch_shapes=[pltpu.VMEM((tm, D_MODEL), BF16)],
        name="ffn_half_step",
        compiler_params=_params(("arbitrary", "arbitrary")),
    )(x, mod, g_pre, g_post, w1, w1, w2)


PROJ_TN = 512
PROJ_CHUNKS = PROJ_TN // LANES
PROJ_TILES = QKV_WIDTH // PROJ_TN


def _rope(y, tab_ref, base, quarter):
    cos, sin_up, sin_dn = tab_ref[base], tab_ref[base + 1], tab_ref[base + 2]
    return (y * cos + pltpu.roll(y, LANES - quarter, 1) * sin_up + pltpu.roll(y, quarter, 1) * sin_dn)


def _head_chunk_kind(u):
    rope_ab, rope_c = (0, HEAD_DIM // 4), (3, C_QK_DIM // 4)
    if u < A_K:
        return 0, rope_ab, Q_GAIN_AB
    if u < A_V:
        return 1, rope_ab, None
    if B_Q <= u < B_K:
        return None, rope_ab, Q_GAIN_AB
    if B_K <= u < B_V:
        return None, rope_ab, None
    if C_Q <= u < C_K:
        return None, rope_c, Q_GAIN_C
    if C_K <= u < C_V:
        return None, rope_c, None
    return None, None, None


def _proj_kernel(*refs, use_rope):
    if use_rope:
        x_ref, mod_ref, gpre_ref, w_ref, qkn_ref, tab_ref, o_ref, ho_ref, h_ref, y0_ref, y1_ref = refs
    else:
        x_ref, mod_ref, gpre_ref, w_ref, qkn_ref, o_ref, ho_ref, h_ref, y0_ref, y1_ref = refs
        tab_ref = None
    y_refs = (y0_ref, y1_ref)
    j = pl.program_id(1)

    def prologue():
        shift = mod_ref[3:4, :]
        scale = mod_ref[4:5, :]
        h = (_rms(x_ref[...], gpre_ref[...]) * (1.0 + scale) + shift).astype(BF16)
        h_ref[...] = h
        ho_ref[...] = h

    def start(t):
        y_refs[t % 2][...] = _dot(h_ref[...], w_ref[...])

    def finish(t):
        y_ref = y_refs[t % 2]
        for c in range(PROJ_CHUNKS):
            norm_row, rope, gain = _head_chunk_kind(t * PROJ_CHUNKS + c)
            v = y_ref[:, c * LANES:(c + 1) * LANES]
            if norm_row is not None:
                v = _rms(v, qkn_ref[norm_row:norm_row + 1, :])
            if rope is not None and use_rope:
                v = _rope(v, tab_ref, *rope)
            if gain is not None:
                v = v * gain
            o_ref[:, c * LANES:(c + 1) * LANES] = v.astype(BF16)

    for step in range(PROJ_TILES + 1):
        @pl.when(j == step)
        def _(step=step):
            if step == 0:
                prologue()
            else:
                finish(step - 1)
            if step < PROJ_TILES:
                start(step)


def _project(x, mod, g_pre, w_in, qk_norm, tables, *, layer, tm, rows_per_group, n_pos):
    rows = x.shape[0]
    use_rope = tables is not None
    tiles_per_group = rows_per_group // tm
    last = PROJ_TILES - 1
    in_specs = [
        pl.BlockSpec((tm, D_MODEL), lambda i, j: (i, 0), pipeline_mode=pl.Buffered(1)),
        pl.BlockSpec((None, N_MOD, D_MODEL), lambda i, j: (i // tiles_per_group, 0, 0)),
        pl.BlockSpec((1, D_MODEL), lambda i, j: (0, 0)),
        pl.BlockSpec((None, D_MODEL, PROJ_TN), lambda i, j: (layer, 0, jnp.minimum(j, last))),
        pl.BlockSpec((2, HEAD_DIM), lambda i, j: (0, 0)),
    ]
    args = [x, mod, g_pre, w_in, qk_norm]
    if use_rope:
        pos_tiles = n_pos // tm
        in_specs.append(pl.BlockSpec((6, tm, LANES), lambda i, j: (0, i % pos_tiles, 0)))
        args.append(tables)
    return pl.pallas_call(
        functools.partial(_proj_kernel, use_rope=use_rope),
        grid=(rows // tm, PROJ_TILES + 1),
        in_specs=in_specs,
        out_specs=[
            pl.BlockSpec((tm, PROJ_TN), lambda i, j: (i, jnp.maximum(j - 1, 0))),
            pl.BlockSpec((tm, D_MODEL), lambda i, j: (i, 0)),
        ],
        out_shape=[
            jax.ShapeDtypeStruct((rows, QKV_WIDTH), BF16),
            jax.ShapeDtypeStruct((rows, D_MODEL), BF16),
        ],
        scratch_shapes=[pltpu.VMEM((tm, D_MODEL), BF16),
                        pltpu.VMEM((tm, PROJ_TN), F32), pltpu.VMEM((tm, PROJ_TN), F32)],
        name="in_projection",
        compiler_params=_params(("arbitrary", "arbitrary")),
    )(*args)


def _rope_tables(n_tok):
    rows = n_tok // GRID_W
    row = jnp.repeat(jnp.arange(rows, dtype=jnp.int32), GRID_W).astype(F32)
    col = jnp.tile(jnp.arange(GRID_W, dtype=jnp.int32), rows).astype(F32)
    out = []
    for rot_dim in (HEAD_DIM, C_QK_DIM):
        d_ax = rot_dim // 2
        inv = ROPE_THETA ** (-jnp.arange(0, d_ax, 2, dtype=F32) / d_ax)
        ang_r = row[:, None] * inv[None, :]
        ang_c = col[:, None] * inv[None, :]
        ang = jnp.concatenate([ang_r, ang_r, ang_c, ang_c], axis=-1)
        reps = LANES // rot_dim
        cos = jnp.tile(jnp.cos(ang), (1, reps))
        sin = jnp.tile(jnp.sin(ang), (1, reps))
        quarter = (jnp.arange(LANES) % rot_dim) // (rot_dim // 4)
        even = (quarter % 2 == 0)[None, :]
        out += [cos, jnp.where(even, -sin, 0.0), jnp.where(even, 0.0, sin)]
    return jnp.stack(out, axis=0)


ATT_TK = 512
ATT_RB = 128


def _softmax_block(s, v, rows, m_ref, l_ref, acc_ref):
    blocks = [s[:, j * LANES:(j + 1) * LANES] for j in range(s.shape[1] // LANES)]
    m_prev = m_ref[rows, :]
    m_new = jnp.maximum(m_prev, jnp.max(functools.reduce(jnp.maximum, blocks), axis=-1, keepdims=True))
    alpha = jnp.exp2(m_prev - m_new)
    p = [jnp.exp2(b - m_new) for b in blocks]
    l_ref[rows, :] = alpha * l_ref[rows, :] + functools.reduce(jnp.add, p)
    pv = _dot(jnp.concatenate(p, axis=1).astype(BF16), v)
    acc_ref[rows, :] = alpha * acc_ref[rows, :] + pv
    m_ref[rows, :] = m_new


def _attend_chunk(qs_ref, k, v, m_ref, l_ref, acc_ref):
    for r in range(qs_ref.shape[0] // ATT_RB):
        rows = slice(r * ATT_RB, (r + 1) * ATT_RB)
        _softmax_block(_dot_nt(qs_ref[rows, :], k), v, rows, m_ref, l_ref, acc_ref)


def _pipelined_keys(qs_ref, kc_ref, vc_ref, k_ref, v_ref, m_ref, l_ref, acc_ref, pipe_refs, n_chunks):
    s_refs, p_refs, a_refs = pipe_refs[0:2], pipe_refs[2:4], pipe_refs[4:6]
    n_rows = qs_ref.shape[0]
    n_ctx = kc_ref.shape[0]
    assert n_chunks % 2 == 0 and n_ctx <= ATT_TK

    def chunk(ref, c):
        return ref[pl.ds(pl.multiple_of(c * ATT_TK, ATT_TK), ATT_TK), :]

    def scores(k, buf):
        s_refs[buf][:, 0:k.shape[0]] = _dot_nt(qs_ref[...], k)

    def softmax(buf, width=ATT_TK):
        for r in range(n_rows // ATT_RB):
            rows = slice(r * ATT_RB, (r + 1) * ATT_RB)
            s = s_refs[buf][rows, 0:width]
            blocks = [s[:, j * LANES:(j + 1) * LANES] for j in range(width // LANES)]
            m_prev = m_ref[rows, :]
            m_new = jnp.maximum(
                m_prev, jnp.max(functools.reduce(jnp.maximum, blocks), axis=-1, keepdims=True))
            alpha = jnp.exp2(m_prev - m_new)
            p = [jnp.exp2(b - m_new) for b in blocks]
            l_ref[rows, :] = alpha * l_ref[rows, :] + functools.reduce(jnp.add, p)
            m_ref[rows, :] = m_new
            a_refs[buf][rows, :] = alpha
            p_refs[buf][rows, 0:width] = jnp.concatenate(p, axis=1).astype(BF16)

    def values(v, buf):
        pv = _dot(p_refs[buf][:, 0:v.shape[0]], v)
        acc_ref[...] = a_refs[buf][...] * acc_ref[...] + pv

    scores(kc_ref[...], 0)
    softmax(0, n_ctx)
    scores(chunk(k_ref, 0), 1)
    values(vc_ref[...], 0)
    softmax(1)
    scores(chunk(k_ref, 1), 0)

    def body(i, carry):
        c = 2 * i + 1
        values(chunk(v_ref, c - 1), 1)
        softmax(0)
        scores(chunk(k_ref, c + 1), 1)
        values(chunk(v_ref, c), 0)
        softmax(1)
        scores(chunk(k_ref, c + 2), 0)
        return carry

    lax.fori_loop(0, n_chunks // 2 - 1, body, 0, unroll=True)
    values(chunk(v_ref, n_chunks - 2), 1)
    softmax(0)
    values(chunk(v_ref, n_chunks - 1), 0)


def _attend_result(l_ref, acc_ref):
    return acc_ref[...] / jnp.sum(l_ref[...], axis=-1, keepdims=True)


def _attn_global_kernel(*refs, diff, with_lat, tq, n_lat, lambda_init):
    refs = list(refs)
    q_ref, kc_ref, vc_ref = refs[:3]
    pos = 3
    if with_lat:
        kl_ref, vl_ref = refs[pos:pos + 2]
        pos += 2
    if diff:
        lam_ref, subln_ref = refs[pos:pos + 2]
        pos += 2
    o_ref, qs_ref, m_ref, l_ref, acc_ref = refs[pos:pos + 5]
    pipe_refs = refs[pos + 5:]

    if diff:
        q = q_ref[...]
        lane = lax.broadcasted_iota(jnp.int32, q.shape, 1)
        qs_ref[0:tq, :] = jnp.where(lane < C_QK_DIM, q, jnp.zeros_like(q))
        qs_ref[tq:2 * tq, :] = jnp.where(lane >= C_QK_DIM, q, jnp.zeros_like(q))
        n_stack = 2
    else:
        n_stack = A_HEADS // A_KV_HEADS
        for g in range(n_stack):
            qs_ref[g * tq:(g + 1) * tq, :] = q_ref[:, g * HEAD_DIM:(g + 1) * HEAD_DIM]

    m_ref[...] = jnp.full(m_ref.shape, NEG_INF, F32)
    l_ref[...] = jnp.zeros(l_ref.shape, F32)
    acc_ref[...] = jnp.zeros(acc_ref.shape, F32)

    if with_lat:
        _pipelined_keys(qs_ref, kc_ref, vc_ref, kl_ref, vl_ref, m_ref, l_ref, acc_ref, pipe_refs,
                        n_lat // ATT_TK)
    else:
        _attend_chunk(qs_ref, kc_ref[...], vc_ref[...], m_ref, l_ref, acc_ref)

    o = _attend_result(l_ref, acc_ref)
    if diff:
        lf = lam_ref[...]
        lam = (jnp.exp(jnp.sum(lf[0:1] * lf[1:2], axis=-1, keepdims=True))
               - jnp.exp(jnp.sum(lf[2:3] * lf[3:4], axis=-1, keepdims=True)) + lambda_init)
        d = o[0:tq] - lam * o[tq:2 * tq]
        o_ref[...] = (_rms(d, subln_ref[...]) * (1.0 - lambda_init)).astype(BF16)
    else:
        for g in range(n_stack):
            o_ref[:, g * HEAD_DIM:(g + 1) * HEAD_DIM] = o[g * tq:(g + 1) * tq].astype(BF16)


def _attn_global(q_src, qkv_ctx, qkv_lat, *, diff, tq, n_q, n_ctx, n_lat, batch,
                 lam=None, subln=None, lambda_init=0.0):
    with_lat = qkv_lat is not None
    q_tiles = n_q // tq
    if diff:
        heads, n_stack, q_w = C_HEADS, 2, HEAD_DIM
        q_col, k_col, v_col = C_Q, C_K, C_V
    else:
        heads, n_stack, q_w = A_KV_HEADS, A_HEADS // A_KV_HEADS, (A_HEADS // A_KV_HEADS) * HEAD_DIM
        q_col, k_col, v_col = A_Q * LANES // q_w, A_K, A_V
    in_specs = [
        pl.BlockSpec((tq, q_w), lambda b, h, t: (b * q_tiles + t, q_col + h)),
        pl.BlockSpec((n_ctx, HEAD_DIM), lambda b, h, t: (b, k_col + h)),
        pl.BlockSpec((n_ctx, HEAD_DIM), lambda b, h, t: (b, v_col + h)),
    ]
    args = [q_src, qkv_ctx, qkv_ctx]
    if with_lat:
        in_specs += [
            pl.BlockSpec((n_lat, HEAD_DIM), lambda b, h, t: (b, k_col + h)),
            pl.BlockSpec((n_lat, HEAD_DIM), lambda b, h, t: (b, v_col + h)),
        ]
        args += [qkv_lat, qkv_lat]
    if diff:
        in_specs += [
            pl.BlockSpec((4, C_QK_DIM), lambda b, h, t: (0, 0)),
            pl.BlockSpec((1, HEAD_DIM), lambda b, h, t: (0, 0)),
        ]
        args += [lam, subln]
    rows = n_stack * tq
    scratch = [
        pltpu.VMEM((rows, HEAD_DIM), BF16),
        pltpu.VMEM((rows, LANES), F32),
        pltpu.VMEM((rows, LANES), F32),
        pltpu.VMEM((rows, HEAD_DIM), F32),
    ]
    if with_lat:
        scratch += ([pltpu.VMEM((rows, ATT_TK), F32)] * 2 + [pltpu.VMEM((rows, ATT_TK), BF16)] * 2
                    + [pltpu.VMEM((rows, LANES), F32)] * 2)
    return pl.pallas_call(
        functools.partial(_attn_global_kernel, diff=diff, with_lat=with_lat, tq=tq, n_lat=n_lat,
                          lambda_init=lambda_init),
        grid=(batch, heads, q_tiles),
        in_specs=in_specs,
        out_specs=pl.BlockSpec((tq, q_w), lambda b, h, t: (b * q_tiles + t, h)),
        out_shape=jax.ShapeDtypeStruct((batch * n_q, BRANCH_WIDTH), BF16),
        scratch_shapes=scratch,
        name="attn_diff" if diff else "attn_global",
        compiler_params=_params(("arbitrary", "arbitrary", "arbitrary")),
    )(*args)


WIN_TQ = 256
WIN_SPAN = WIN_TQ + 2 * WINDOW


def _attn_window_kernel(*refs, with_lat, n_lat):
    if with_lat:
        sink_ref, q_ref, kc_ref, vc_ref, kl_ref, vl_ref, o_ref, qs_ref, sc_ref, pc_ref, linv_ref, \
            sw_ref, pw_ref = refs
    else:
        sink_ref, q_ref, kc_ref, vc_ref, o_ref, qs_ref, sc_ref, pc_ref, linv_ref = refs
    tq = WIN_TQ
    n_stack = B_HEADS // B_KV_HEADS
    h = pl.program_id(1)
    t = pl.program_id(2)
    for g in range(n_stack):
        qs_ref[g * tq:(g + 1) * tq, :] = q_ref[:, g * HEAD_DIM:(g + 1) * HEAD_DIM]

    sc_ref[...] = _dot_nt(qs_ref[...], kc_ref[...])
    if with_lat:
        start = pl.multiple_of(jnp.clip(t * tq - WINDOW, 0, n_lat - WIN_SPAN), WINDOW)
        sw_ref[...] = _dot_nt(qs_ref[...], kl_ref[pl.ds(start, WIN_SPAN), :])
        row = lax.broadcasted_iota(jnp.int32, (tq, WIN_SPAN), 0)
        col = lax.broadcasted_iota(jnp.int32, (tq, WIN_SPAN), 1)
        in_band = jnp.abs(start + col - (t * tq + row)) <= WINDOW

    def lane_blocks(s):
        return [s[:, j * LANES:(j + 1) * LANES] for j in range(s.shape[1] // LANES)]

    for g in range(n_stack):
        rows = slice(g * tq, (g + 1) * tq)
        blocks = lane_blocks(sc_ref[rows, :])
        n_ctx_blocks = len(blocks)
        if with_lat:
            blocks += lane_blocks(jnp.where(in_band, sw_ref[rows, :], NEG_INF))
        sink = jnp.full((tq, LANES), sink_ref[h * n_stack + g] * LOG2_E, F32)
        m = jnp.maximum(sink, jnp.max(functools.reduce(jnp.maximum, blocks), axis=-1, keepdims=True))
        p = [jnp.exp2(b - m) for b in blocks]
        denom = jnp.sum(functools.reduce(jnp.add, p), axis=-1, keepdims=True) + jnp.exp2(sink - m)
        linv_ref[rows, :] = 1.0 / denom
        pc_ref[rows, :] = jnp.concatenate(p[:n_ctx_blocks], axis=1).astype(BF16)
        if with_lat:
            pw_ref[rows, :] = jnp.concatenate(p[n_ctx_blocks:], axis=1).astype(BF16)

    o = _dot(pc_ref[...], vc_ref[...])
    if with_lat:
        o = o + _dot(pw_ref[...], vl_ref[pl.ds(start, WIN_SPAN), :])
    o = o * linv_ref[...]
    for g in range(n_stack):
        o_ref[:, g * HEAD_DIM:(g + 1) * HEAD_DIM] = o[g * tq:(g + 1) * tq].astype(BF16)


def _attn_window(q_src, qkv_ctx, qkv_lat, sink, *, n_q, n_ctx, n_lat, batch):
    with_lat = qkv_lat is not None
    tq = WIN_TQ
    q_tiles = n_q // tq
    n_stack = B_HEADS // B_KV_HEADS
    q_w = n_stack * HEAD_DIM
    q_col = B_Q * LANES // q_w
    in_specs = [
        pl.BlockSpec(memory_space=pltpu.SMEM),
        pl.BlockSpec((tq, q_w), lambda b, h, t: (b * q_tiles + t, q_col + h)),
        pl.BlockSpec((n_ctx, HEAD_DIM), lambda b, h, t: (b, B_K + h)),
        pl.BlockSpec((n_ctx, HEAD_DIM), lambda b, h, t: (b, B_V + h)),
    ]
    args = [sink, q_src, qkv_ctx, qkv_ctx]
    if with_lat:
        in_specs += [
            pl.BlockSpec((n_lat, HEAD_DIM), lambda b, h, t: (b, B_K + h)),
            pl.BlockSpec((n_lat, HEAD_DIM), lambda b, h, t: (b, B_V + h)),
        ]
        args += [qkv_lat, qkv_lat]
    rows = n_stack * tq
    scratch = [
        pltpu.VMEM((rows, HEAD_DIM), BF16),
        pltpu.VMEM((rows, n_ctx), F32),
        pltpu.VMEM((rows, n_ctx), BF16),
        pltpu.VMEM((rows, LANES), F32),
    ]
    if with_lat:
        scratch += [pltpu.VMEM((rows, WIN_SPAN), F32), pltpu.VMEM((rows, WIN_SPAN), BF16)]
    return pl.pallas_call(
        functools.partial(_attn_window_kernel, with_lat=with_lat, n_lat=n_lat),
        grid=(batch, B_KV_HEADS, q_tiles),
        in_specs=in_specs,
        out_specs=pl.BlockSpec((tq, q_w), lambda b, h, t: (b * q_tiles + t, h)),
        out_shape=jax.ShapeDtypeStruct((batch * n_q, BRANCH_WIDTH), BF16),
        scratch_shapes=scratch,
        name="attn_window",
        compiler_params=_params(("arbitrary", "arbitrary", "arbitrary")),
    )(*args)


MERGE_TN = 1024


def _merge_kernel(h_ref, oa_ref, ob_ref, oc_ref, wg_ref, wb_ref, y_ref, acc_ref):
    r = pl.program_id(2)
    gate = jax.nn.sigmoid(_dot(h_ref[...], wg_ref[...]))
    for k, o_ref in enumerate((oa_ref, ob_ref, oc_ref)):
        @pl.when(r == k)
        def _(o_ref=o_ref, k=k):
            part = gate * _dot(o_ref[...], wb_ref[...])
            if k == 0:
                acc_ref[...] = part
            elif k < N_BRANCH - 1:
                acc_ref[...] += part
            else:
                y_ref[...] = (acc_ref[...] + part).astype(BF16)


def _merge(h, o_a, o_b, o_c, w_in, w_branch, *, layer, tm):
    rows = h.shape[0]
    n_col = D_MODEL // MERGE_TN
    gate_col = QKV_WIDTH // MERGE_TN
    o_spec = pl.BlockSpec((tm, BRANCH_WIDTH), lambda i, n, r: (i, 0))
    return pl.pallas_call(
        _merge_kernel,
        grid=(rows // tm, n_col, N_BRANCH),
        in_specs=[
            pl.BlockSpec((tm, D_MODEL), lambda i, n, r: (i, 0)),
            o_spec, o_spec, o_spec,
            pl.BlockSpec((None, D_MODEL, MERGE_TN), lambda i, n, r: (layer, 0, gate_col + r * n_col + n)),
            pl.BlockSpec((None, None, BRANCH_WIDTH, MERGE_TN), lambda i, n, r: (layer, r, 0, n)),
        ],
        out_specs=pl.BlockSpec((tm, MERGE_TN), lambda i, n, r: (i, n)),
        out_shape=jax.ShapeDtypeStruct((rows, D_MODEL), BF16),
        scratch_shapes=[pltpu.VMEM((tm, MERGE_TN), F32)],
        name="branch_merge",
        compiler_params=_params(("arbitrary", "arbitrary", "arbitrary")),
    )(h, o_a, o_b, o_c, w_in, w_branch)


def _out_kernel(y_ref, x_ref, mod_ref, gpost_ref, w_ref, o_ref):
    z = _dot(y_ref[...], w_ref[...])
    o_ref[...] = x_ref[...] + mod_ref[5:6, :] * _rms(z, gpost_ref[...])


def _out_project(y, x, mod, g_post, w_out, *, layer, tm, rows_per_group):
    rows = x.shape[0]
    tiles_per_group = rows_per_group // tm
    return pl.pallas_call(
        _out_kernel,
        grid=(rows // tm,),
        in_specs=[
            pl.BlockSpec((tm, D_MODEL), lambda i: (i, 0)),
            pl.BlockSpec((tm, D_MODEL), lambda i: (i, 0)),
            pl.BlockSpec((None, N_MOD, D_MODEL), lambda i: (i // tiles_per_group, 0, 0)),
            pl.BlockSpec((1, D_MODEL), lambda i: (0, 0)),
            pl.BlockSpec((None, D_MODEL, D_MODEL), lambda i: (layer, 0, 0)),
        ],
        out_specs=pl.BlockSpec((tm, D_MODEL), lambda i: (i, 0)),
        out_shape=jax.ShapeDtypeStruct((rows, D_MODEL), F32),
        name="out_projection",
        compiler_params=_params(("arbitrary",)),
    )(y, x, mod, g_post, w_out)


def kernel(x, c, ctx, c_ctx, w_ada, b_ada, norm_pre, norm_post, w_ff_in, w_ff_out, w_in, qk_norm_a,
           sink_b, lam_c, subln_c, w_branch, w_out):
    batch, n_lat, _ = x.shape
    n_ctx = ctx.shape[1]
    depth = w_ada.shape[0]
    assert (depth, x.shape[2]) == (DEPTH, D_MODEL)

    tm_lat_ffn, tm_lat_proj, tm_lat_merge, tm_lat_out = 512, 1024, 1024, 512
    tm_ctx = batch * n_ctx

    cvec = jnp.concatenate([c, c_ctx[None, :]], axis=0)
    mod_all = _modulation(cvec, w_ada, b_ada)
    tables = _rope_tables(n_lat)

    w_ff_in_h = w_ff_in.astype(BF16)
    w_ff_out_h = w_ff_out.astype(BF16)
    w_in_h = w_in.astype(BF16)
    w_branch_h = w_branch.astype(BF16)
    w_out_h = w_out.astype(BF16)

    x_lat = x.reshape(batch * n_lat, D_MODEL)
    x_ctx = ctx.reshape(batch * n_ctx, D_MODEL)

    for l in range(depth):
        last = l == depth - 1
        lambda_init = 0.8 - 0.6 * math.exp(-0.3 * l)
        mod_lat = mod_all[l, 0:batch].reshape(batch, N_MOD, D_MODEL)
        mod_ctx = mod_all[l, batch:batch + 1].reshape(1, N_MOD, D_MODEL)
        g_pre = [norm_pre[l, s][None, :] for s in range(3)]
        g_post = [norm_post[l, s][None, :] for s in range(3)]
        lat = dict(rows_per_group=n_lat)
        cx = dict(rows_per_group=tm_ctx)

        ffn1 = dict(layer=l, which=0, mod_row=0)
        ffn2 = dict(layer=l, which=1, mod_row=6)
        x_ctx = _ffn_half_step(x_ctx, mod_ctx, g_pre[0], g_post[0], w_ff_in_h, w_ff_out_h,
                               tm=tm_ctx, **ffn1, **cx)
        x_lat = _ffn_half_step(x_lat, mod_lat, g_pre[0], g_post[0], w_ff_in_h, w_ff_out_h,
                               tm=tm_lat_ffn, **ffn1, **lat)

        qkv_ctx, h_ctx = _project(x_ctx, mod_ctx, g_pre[1], w_in_h, qk_norm_a[l], None,
                                  layer=l, tm=tm_ctx, n_pos=n_lat, **cx)
        qkv_lat, h_lat = _project(x_lat, mod_lat, g_pre[1], w_in_h, qk_norm_a[l], tables,
                                  layer=l, tm=tm_lat_proj, n_pos=n_lat, **lat)

        dims = dict(n_ctx=n_ctx, n_lat=n_lat, batch=batch)
        diff_args = dict(lam=lam_c[l], subln=subln_c[l][None, :], lambda_init=lambda_init)
        a_lat = _attn_global(qkv_lat, qkv_ctx, qkv_lat, diff=False, tq=256, n_q=n_lat, **dims)
        b_lat = _attn_window(qkv_lat, qkv_ctx, qkv_lat, sink_b[l], n_q=n_lat, **dims)
        c_lat = _attn_global(qkv_lat, qkv_ctx, qkv_lat, diff=True, tq=512, n_q=n_lat, **dims, **diff_args)
        y_lat = _merge(h_lat, a_lat, b_lat, c_lat, w_in_h, w_branch_h, layer=l, tm=tm_lat_merge)
        x_lat = _out_project(y_lat, x_lat, mod_lat, g_post[1], w_out_h, layer=l, tm=tm_lat_out, **lat)
        x_lat = _ffn_half_step(x_lat, mod_lat, g_pre[2], g_post[2], w_ff_in_h, w_ff_out_h,
                               tm=tm_lat_ffn, **ffn2, **lat)

        if not last:
            a_ctx = _attn_global(qkv_ctx, qkv_ctx, None, diff=False, tq=n_ctx, n_q=n_ctx, **dims)
            b_ctx = _attn_window(qkv_ctx, qkv_ctx, None, sink_b[l], n_q=n_ctx, **dims)
            c_ctx_o = _attn_global(qkv_ctx, qkv_ctx, None, diff=True, tq=n_ctx, n_q=n_ctx, **dims, **diff_args)
            y_ctx = _merge(h_ctx, a_ctx, b_ctx, c_ctx_o, w_in_h, w_branch_h, layer=l, tm=tm_ctx)
            x_ctx = _out_project(y_ctx, x_ctx, mod_ctx, g_post[1], w_out_h, layer=l, tm=tm_ctx, **cx)
            x_ctx = _ffn_half_step(x_ctx, mod_ctx, g_pre[2], g_post[2], w_ff_in_h, w_ff_out_h,
                                   tm=tm_ctx, **ffn2, **cx)

    return x_lat.reshape(batch, n_lat, D_MODEL)
```

```python
import functools
import math

import jax
import jax.numpy as jnp
from jax import lax
from jax.experimental import pallas as pl
from jax.experimental.pallas import tpu as pltpu

D_MODEL = 2048
DEPTH = 4
GRID_W = 64
HEAD_DIM = 128
A_HEADS = 8
A_KV_HEADS = 2
B_HEADS = 8
B_KV_HEADS = 2
C_HEADS = 8
C_QK_DIM = HEAD_DIM // 2
BRANCH_WIDTH = 1024
N_BRANCH = 3
WINDOW = 128
D_FF = 5632
HALF_STEP = 0.5
ROPE_THETA = 10000.0
NORM_EPS = 1e-6
NEG_INF = -1e30
N_MOD = 9
QKV_WIDTH = 6144
LANES = 128
SUBLANES = 8
VMEM_LIMIT = 56 * 1024 * 1024

A_Q, A_K, A_V = 0, 8, 10
B_Q, B_K, B_V = 12, 20, 22
C_Q, C_K, C_V = 24, 32, 40

F32 = jnp.float32
BF16 = jnp.bfloat16

LOG2_E = math.log2(math.e)
Q_GAIN_AB = HEAD_DIM ** -0.5 * LOG2_E
Q_GAIN_C = C_QK_DIM ** -0.5 * LOG2_E


def _params(sem, vmem=VMEM_LIMIT):
    return pltpu.CompilerParams(dimension_semantics=sem, vmem_limit_bytes=vmem)


def _rms(x, g):
    ms = jnp.mean(x * x, axis=-1, keepdims=True)
    return x * lax.rsqrt(ms + NORM_EPS) * g


def _dot(a, b):
    return jnp.dot(a, b, preferred_element_type=F32)


def _dot_nt(a, b):
    return lax.dot_general(a, b, (((1,), (1,)), ((), ())), preferred_element_type=F32)


MOD_TN = 1024


def _mod_kernel(c_ref, w_ref, b_ref, o_ref):
    n_rep = MOD_TN // LANES

    def body(kc, accs):
        r0 = pl.multiple_of(kc * SUBLANES, SUBLANES)
        wk = w_ref[0, pl.ds(r0, SUBLANES), :]
        out = []
        for m in range(3):
            cm = c_ref[m, pl.ds(r0, SUBLANES), :]
            sm = cm * jax.nn.sigmoid(cm)
            out.append(accs[m] + jnp.concatenate([sm] * n_rep, axis=1) * wk)
        return tuple(out)

    zero = jnp.zeros((SUBLANES, MOD_TN), F32)
    accs = lax.fori_loop(0, D_MODEL // SUBLANES, body, (zero, zero, zero), unroll=4)
    bias = b_ref[0]
    for m in range(3):
        o_ref[0, m:m + 1, :] = jnp.sum(accs[m], axis=0, keepdims=True) + bias
    o_ref[0, 3:SUBLANES, :] = jnp.zeros((SUBLANES - 3, MOD_TN), F32)


def _modulation(cvec, w_ada, b_ada):
    width = N_MOD * D_MODEL
    c_b = jnp.broadcast_to(cvec[:, :, None], (3, D_MODEL, LANES))
    return pl.pallas_call(
        _mod_kernel,
        grid=(DEPTH, width // MOD_TN),
        in_specs=[
            pl.BlockSpec((3, D_MODEL, LANES), lambda l, j: (0, 0, 0)),
            pl.BlockSpec((1, D_MODEL, MOD_TN), lambda l, j: (l, 0, j)),
            pl.BlockSpec((1, 1, MOD_TN), lambda l, j: (l, 0, j)),
        ],
        out_specs=pl.BlockSpec((1, SUBLANES, MOD_TN), lambda l, j: (l, 0, j)),
        out_shape=jax.ShapeDtypeStruct((DEPTH, SUBLANES, width), F32),
        name="adaln_modulation",
        compiler_params=_params(("arbitrary", "arbitrary")),
    )(c_b, w_ada, b_ada.reshape(DEPTH, 1, width))


FFN_TF = 512


def _ffn_kernel(x_ref, mod_ref, gpre_ref, gpost_ref, w1g_ref, w1u_ref, w2_ref, o_ref, h_ref, *, mod_row):
    j = pl.program_id(1)
    last = D_FF // FFN_TF - 1

    def prologue():
        shift = mod_ref[mod_row:mod_row + 1, :]
        scale = mod_ref[mod_row + 1:mod_row + 2, :]
        h = _rms(x_ref[...], gpre_ref[...]) * (1.0 + scale) + shift
        h_ref[...] = h.astype(BF16)

    def hidden_tile(first):
        h = h_ref[...]
        g = _dot(h, w1g_ref[...])
        u = _dot(h, w1u_ref[...])
        a = (g * jax.nn.sigmoid(g)) * u
        part = _dot(a.astype(BF16), w2_ref[...])
        if first:
            o_ref[...] = part
        else:
            o_ref[...] += part

    def epilogue():
        gate = mod_ref[mod_row + 2:mod_row + 3, :]
        y = _rms(o_ref[...], gpost_ref[...])
        o_ref[...] = x_ref[...] + HALF_STEP * gate * y

    @pl.when(j == 0)
    def _():
        prologue()
        hidden_tile(True)

    @pl.when(jnp.logical_and(j > 0, j < last))
    def _():
        hidden_tile(False)

    @pl.when(j == last)
    def _():
        hidden_tile(False)
        epilogue()


def _ffn_half_step(x, mod, g_pre, g_post, w1, w2, *, layer, which, mod_row, tm, rows_per_group):
    rows = x.shape[0]
    nf = D_FF // FFN_TF
    tiles_per_group = rows_per_group // tm
    return pl.pallas_call(
        functools.partial(_ffn_kernel, mod_row=mod_row),
        grid=(rows // tm, nf),
        in_specs=[
            pl.BlockSpec((tm, D_MODEL), lambda i, j: (i, 0)),
            pl.BlockSpec((None, N_MOD, D_MODEL), lambda i, j: (i // tiles_per_group, 0, 0)),
            pl.BlockSpec((1, D_MODEL), lambda i, j: (0, 0)),
            pl.BlockSpec((1, D_MODEL), lambda i, j: (0, 0)),
            pl.BlockSpec((None, None, D_MODEL, FFN_TF), lambda i, j: (layer, which, 0, j)),
            pl.BlockSpec((None, None, D_MODEL, FFN_TF), lambda i, j: (layer, which, 0, nf + j)),
            pl.BlockSpec((None, None, FFN_TF, D_MODEL), lambda i, j: (layer, which, j, 0)),
        ],
        out_specs=pl.BlockSpec((tm, D_MODEL), lambda i, j: (i, 0)),
        out_shape=jax.ShapeDtypeStruct((rows, D_MODEL), F32),
        scratch_shapes=[pltpu.VMEM((tm, D_MODEL), BF16)],
        name="ffn_half_step",
        compiler_params=_params(("arbitrary", "arbitrary")),
    )(x, mod, g_pre, g_post, w1, w1, w2)


PROJ_TN = 512
PROJ_CHUNKS = PROJ_TN // LANES
PROJ_TILES = QKV_WIDTH // PROJ_TN


def _rope(y, tab_ref, base, quarter):
    cos, sin_up, sin_dn = tab_ref[base], tab_ref[base + 1], tab_ref[base + 2]
    return (y * cos + pltpu.roll(y, LANES - quarter, 1) * sin_up + pltpu.roll(y, quarter, 1) * sin_dn)


def _head_chunk_kind(u):
    rope_ab, rope_c = (0, HEAD_DIM // 4), (3, C_QK_DIM // 4)
    if u < A_K:
        return 0, rope_ab, Q_GAIN_AB
    if u < A_V:
        return 1, rope_ab, None
    if B_Q <= u < B_K:
        return None, rope_ab, Q_GAIN_AB
    if B_K <= u < B_V:
        return None, rope_ab, None
    if C_Q <= u < C_K:
        return None, rope_c, Q_GAIN_C
    if C_K <= u < C_V:
        return None, rope_c, None
    return None, None, None


def _proj_kernel(*refs, use_rope):
    if use_rope:
        x_ref, mod_ref, gpre_ref, w_ref, qkn_ref, tab_ref, o_ref, ho_ref, h_ref, y0_ref, y1_ref = refs
    else:
        x_ref, mod_ref, gpre_ref, w_ref, qkn_ref, o_ref, ho_ref, h_ref, y0_ref, y1_ref = refs
        tab_ref = None
    y_refs = (y0_ref, y1_ref)
    j = pl.program_id(1)

    def prologue():
        shift = mod_ref[3:4, :]
        scale = mod_ref[4:5, :]
        h = (_rms(x_ref[...], gpre_ref[...]) * (1.0 + scale) + shift).astype(BF16)
        h_ref[...] = h
        ho_ref[...] = h

    def start(t):
        y_refs[t % 2][...] = _dot(h_ref[...], w_ref[...])

    def finish(t):
        y_ref = y_refs[t % 2]
        for c in range(PROJ_CHUNKS):
            norm_row, rope, gain = _head_chunk_kind(t * PROJ_CHUNKS + c)
            v = y_ref[:, c * LANES:(c + 1) * LANES]
            if norm_row is not None:
                v = _rms(v, qkn_ref[norm_row:norm_row + 1, :])
            if rope is not None and use_rope:
                v = _rope(v, tab_ref, *rope)
            if gain is not None:
                v = v * gain
            o_ref[:, c * LANES:(c + 1) * LANES] = v.astype(BF16)

    for step in range(PROJ_TILES + 1):
        @pl.when(j == step)
        def _(step=step):
            if step == 0:
                prologue()
            else:
                finish(step - 1)
            if step < PROJ_TILES:
                start(step)


def _project(x, mod, g_pre, w_in, qk_norm, tables, *, layer, tm, rows_per_group, n_pos):
    rows = x.shape[0]
    use_rope = tables is not None
    tiles_per_group = rows_per_group // tm
    last = PROJ_TILES - 1
    in_specs = [
        pl.BlockSpec((tm, D_MODEL), lambda i, j: (i, 0), pipeline_mode=pl.Buffered(1)),
        pl.BlockSpec((None, N_MOD, D_MODEL), lambda i, j: (i // tiles_per_group, 0, 0)),
        pl.BlockSpec((1, D_MODEL), lambda i, j: (0, 0)),
        pl.BlockSpec((None, D_MODEL, PROJ_TN), lambda i, j: (layer, 0, jnp.minimum(j, last))),
        pl.BlockSpec((2, HEAD_DIM), lambda i, j: (0, 0)),
    ]
    args = [x, mod, g_pre, w_in, qk_norm]
    if use_rope:
        pos_tiles = n_pos // tm
        in_specs.append(pl.BlockSpec((6, tm, LANES), lambda i, j: (0, i % pos_tiles, 0)))
        args.append(tables)
    return pl.pallas_call(
        functools.partial(_proj_kernel, use_rope=use_rope),
        grid=(rows // tm, PROJ_TILES + 1),
        in_specs=in_specs,
        out_specs=[
            pl.BlockSpec((tm, PROJ_TN), lambda i, j: (i, jnp.maximum(j - 1, 0))),
            pl.BlockSpec((tm, D_MODEL), lambda i, j: (i, 0)),
        ],
        out_shape=[
            jax.ShapeDtypeStruct((rows, QKV_WIDTH), BF16),
            jax.ShapeDtypeStruct((rows, D_MODEL), BF16),
        ],
        scratch_shapes=[pltpu.VMEM((tm, D_MODEL), BF16),
                        pltpu.VMEM((tm, PROJ_TN), F32), pltpu.VMEM((tm, PROJ_TN), F32)],
        name="in_projection",
        compiler_params=_params(("arbitrary", "arbitrary")),
    )(*args)


def _rope_tables(n_tok):
    rows = n_tok // GRID_W
    row = jnp.repeat(jnp.arange(rows, dtype=jnp.int32), GRID_W).astype(F32)
    col = jnp.tile(jnp.arange(GRID_W, dtype=jnp.int32), rows).astype(F32)
    out = []
    for rot_dim in (HEAD_DIM, C_QK_DIM):
        d_ax = rot_dim // 2
        inv = ROPE_THETA ** (-jnp.arange(0, d_ax, 2, dtype=F32) / d_ax)
        ang_r = row[:, None] * inv[None, :]
        ang_c = col[:, None] * inv[None, :]
        ang = jnp.concatenate([ang_r, ang_r, ang_c, ang_c], axis=-1)
        reps = LANES // rot_dim
        cos = jnp.tile(jnp.cos(ang), (1, reps))
        sin = jnp.tile(jnp.sin(ang), (1, reps))
        quarter = (jnp.arange(LANES) % rot_dim) // (rot_dim // 4)
        even = (quarter % 2 == 0)[None, :]
        out += [cos, jnp.where(even, -sin, 0.0), jnp.where(even, 0.0, sin)]
    return jnp.stack(out, axis=0)


ATT_TK = 512
ATT_RB = 128


def _softmax_block(s, v, rows, m_ref, l_ref, acc_ref):
    blocks = [s[:, j * LANES:(j + 1) * LANES] for j in range(s.shape[1] // LANES)]
    m_prev = m_ref[rows, :]
    m_new = jnp.maximum(m_prev, jnp.max(functools.reduce(jnp.maximum, blocks), axis=-1, keepdims=True))
    alpha = jnp.exp2(m_prev - m_new)
    p = [jnp.exp2(b - m_new) for b in blocks]
    l_ref[rows, :] = alpha * l_ref[rows, :] + functools.reduce(jnp.add, p)
    pv = _dot(jnp.concatenate(p, axis=1).astype(BF16), v)
    acc_ref[rows, :] = alpha * acc_ref[rows, :] + pv
    m_ref[rows, :] = m_new


def _attend_chunk(qs_ref, k, v, m_ref, l_ref, acc_ref):
    for r in range(qs_ref.shape[0] // ATT_RB):
        rows = slice(r * ATT_RB, (r + 1) * ATT_RB)
        _softmax_block(_dot_nt(qs_ref[rows, :], k), v, rows, m_ref, l_ref, acc_ref)


def _pipelined_keys(qs_ref, kc_ref, vc_ref, k_ref, v_ref, m_ref, l_ref, acc_ref, pipe_refs, n_chunks):
    s_refs, p_refs, a_refs = pipe_refs[0:2], pipe_refs[2:4], pipe_refs[4:6]
    n_rows = qs_ref.shape[0]
    n_ctx = kc_ref.shape[0]
    assert n_chunks % 2 == 0 and n_ctx <= ATT_TK

    def chunk(ref, c):
        return ref[pl.ds(pl.multiple_of(c * ATT_TK, ATT_TK), ATT_TK), :]

    def scores(k, buf):
        s_refs[buf][:, 0:k.shape[0]] = _dot_nt(qs_ref[...], k)

    def softmax(buf, width=ATT_TK):
        for r in range(n_rows // ATT_RB):
            rows = slice(r * ATT_RB, (r + 1) * ATT_RB)
            s = s_refs[buf][rows, 0:width]
            blocks = [s[:, j * LANES:(j + 1) * LANES] for j in range(width // LANES)]
            m_prev = m_ref[rows, :]
            m_new = jnp.maximum(
                m_prev, jnp.max(functools.reduce(jnp.maximum, blocks), axis=-1, keepdims=True))
            alpha = jnp.exp2(m_prev - m_new)
            p = [jnp.exp2(b - m_new) for b in blocks]
            l_ref[rows, :] = alpha * l_ref[rows, :] + functools.reduce(jnp.add, p)
            m_ref[rows, :] = m_new
            a_refs[buf][rows, :] = alpha
            p_refs[buf][rows, 0:width] = jnp.concatenate(p, axis=1).astype(BF16)

    def values(v, buf):
        pv = _dot(p_refs[buf][:, 0:v.shape[0]], v)
        acc_ref[...] = a_refs[buf][...] * acc_ref[...] + pv

    scores(kc_ref[...], 0)
    softmax(0, n_ctx)
    scores(chunk(k_ref, 0), 1)
    values(vc_ref[...], 0)
    softmax(1)
    scores(chunk(k_ref, 1), 0)

    def body(i, carry):
        c = 2 * i + 1
        values(chunk(v_ref, c - 1), 1)
        softmax(0)
        scores(chunk(k_ref, c + 1), 1)
        values(chunk(v_ref, c), 0)
        softmax(1)
        scores(chunk(k_ref, c + 2), 0)
        return carry

    lax.fori_loop(0, n_chunks // 2 - 1, body, 0, unroll=True)
    values(chunk(v_ref, n_chunks - 2), 1)
    softmax(0)
    values(chunk(v_ref, n_chunks - 1), 0)


def _attend_result(l_ref, acc_ref):
    return acc_ref[...] / jnp.sum(l_ref[...], axis=-1, keepdims=True)


def _attn_global_kernel(*refs, diff, with_lat, tq, n_lat, lambda_init):
    refs = list(refs)
    q_ref, kc_ref, vc_ref = refs[:3]
    pos = 3
    if with_lat:
        kl_ref, vl_ref = refs[pos:pos + 2]
        pos += 2
    if diff:
        lam_ref, subln_ref = refs[pos:pos + 2]
        pos += 2
    o_ref, qs_ref, m_ref, l_ref, acc_ref = refs[pos:pos + 5]
    pipe_refs = refs[pos + 5:]

    if diff:
        q = q_ref[...]
        lane = lax.broadcasted_iota(jnp.int32, q.shape, 1)
        qs_ref[0:tq, :] = jnp.where(lane < C_QK_DIM, q, jnp.zeros_like(q))
        qs_ref[tq:2 * tq, :] = jnp.where(lane >= C_QK_DIM, q, jnp.zeros_like(q))
        n_stack = 2
    else:
        n_stack = A_HEADS // A_KV_HEADS
        for g in range(n_stack):
            qs_ref[g * tq:(g + 1) * tq, :] = q_ref[:, g * HEAD_DIM:(g + 1) * HEAD_DIM]

    m_ref[...] = jnp.full(m_ref.shape, NEG_INF, F32)
    l_ref[...] = jnp.zeros(l_ref.shape, F32)
    acc_ref[...] = jnp.zeros(acc_ref.shape, F32)

    if with_lat:
        _pipelined_keys(qs_ref, kc_ref, vc_ref, kl_ref, vl_ref, m_ref, l_ref, acc_ref, pipe_refs,
                        n_lat // ATT_TK)
    else:
        _attend_chunk(qs_ref, kc_ref[...], vc_ref[...], m_ref, l_ref, acc_ref)

    o = _attend_result(l_ref, acc_ref)
    if diff:
        lf = lam_ref[...]
        lam = (jnp.exp(jnp.sum(lf[0:1] * lf[1:2], axis=-1, keepdims=True))
               - jnp.exp(jnp.sum(lf[2:3] * lf[3:4], axis=-1, keepdims=True)) + lambda_init)
        d = o[0:tq] - lam * o[tq:2 * tq]
        o_ref[...] = (_rms(d, subln_ref[...]) * (1.0 - lambda_init)).astype(BF16)
    else:
        for g in range(n_stack):
            o_ref[:, g * HEAD_DIM:(g + 1) * HEAD_DIM] = o[g * tq:(g + 1) * tq].astype(BF16)


def _attn_global(q_src, qkv_ctx, qkv_lat, *, diff, tq, n_q, n_ctx, n_lat, batch,
                 lam=None, subln=None, lambda_init=0.0):
    with_lat = qkv_lat is not None
    q_tiles = n_q // tq
    if diff:
        heads, n_stack, q_w = C_HEADS, 2, HEAD_DIM
        q_col, k_col, v_col = C_Q, C_K, C_V
    else:
        heads, n_stack, q_w = A_KV_HEADS, A_HEADS // A_KV_HEADS, (A_HEADS // A_KV_HEADS) * HEAD_DIM
        q_col, k_col, v_col = A_Q * LANES // q_w, A_K, A_V
    in_specs = [
        pl.BlockSpec((tq, q_w), lambda b, h, t: (b * q_tiles + t, q_col + h)),
        pl.BlockSpec((n_ctx, HEAD_DIM), lambda b, h, t: (b, k_col + h)),
        pl.BlockSpec((n_ctx, HEAD_DIM), lambda b, h, t: (b, v_col + h)),
    ]
    args = [q_src, qkv_ctx, qkv_ctx]
    if with_lat:
        in_specs += [
            pl.BlockSpec((n_lat, HEAD_DIM), lambda b, h, t: (b, k_col + h)),
            pl.BlockSpec((n_lat, HEAD_DIM), lambda b, h, t: (b, v_col + h)),
        ]
        args += [qkv_lat, qkv_lat]
    if diff:
        in_specs += [
            pl.BlockSpec((4, C_QK_DIM), lambda b, h, t: (0, 0)),
            pl.BlockSpec((1, HEAD_DIM), lambda b, h, t: (0, 0)),
        ]
        args += [lam, subln]
    rows = n_stack * tq
    scratch = [
        pltpu.VMEM((rows, HEAD_DIM), BF16),
        pltpu.VMEM((rows, LANES), F32),
        pltpu.VMEM((rows, LANES), F32),
        pltpu.VMEM((rows, HEAD_DIM), F32),
    ]
    if with_lat:
        scratch += ([pltpu.VMEM((rows, ATT_TK), F32)] * 2 + [pltpu.VMEM((rows, ATT_TK), BF16)] * 2
                    + [pltpu.VMEM((rows, LANES), F32)] * 2)
    return pl.pallas_call(
        functools.partial(_attn_global_kernel, diff=diff, with_lat=with_lat, tq=tq, n_lat=n_lat,
                          lambda_init=lambda_init),
        grid=(batch, heads, q_tiles),
        in_specs=in_specs,
        out_specs=pl.BlockSpec((tq, q_w), lambda b, h, t: (b * q_tiles + t, h)),
        out_shape=jax.ShapeDtypeStruct((batch * n_q, BRANCH_WIDTH), BF16),
        scratch_shapes=scratch,
        name="attn_diff" if diff else "attn_global",
        compiler_params=_params(("arbitrary", "arbitrary", "arbitrary")),
    )(*args)


WIN_TQ = 256
WIN_SPAN = WIN_TQ + 2 * WINDOW


def _attn_window_kernel(*refs, with_lat, n_lat):
    if with_lat:
        sink_ref, q_ref, kc_ref, vc_ref, kl_ref, vl_ref, o_ref, qs_ref, sc_ref, pc_ref, linv_ref, \
            sw_ref, pw_ref = refs
    else:
        sink_ref, q_ref, kc_ref, vc_ref, o_ref, qs_ref, sc_ref, pc_ref, linv_ref = refs
    tq = WIN_TQ
    n_stack = B_HEADS // B_KV_HEADS
    h = pl.program_id(1)
    t = pl.program_id(2)
    for g in range(n_stack):
        qs_ref[g * tq:(g + 1) * tq, :] = q_ref[:, g * HEAD_DIM:(g + 1) * HEAD_DIM]

    sc_ref[...] = _dot_nt(qs_ref[...], kc_ref[...])
    if with_lat:
        start = pl.multiple_of(jnp.clip(t * tq - WINDOW, 0, n_lat - WIN_SPAN), WINDOW)
        sw_ref[...] = _dot_nt(qs_ref[...], kl_ref[pl.ds(start, WIN_SPAN), :])
        row = lax.broadcasted_iota(jnp.int32, (tq, WIN_SPAN), 0)
        col = lax.broadcasted_iota(jnp.int32, (tq, WIN_SPAN), 1)
        in_band = jnp.abs(start + col - (t * tq + row)) <= WINDOW

    def lane_blocks(s):
        return [s[:, j * LANES:(j + 1) * LANES] for j in range(s.shape[1] // LANES)]

    for g in range(n_stack):
        rows = slice(g * tq, (g + 1) * tq)
        blocks = lane_blocks(sc_ref[rows, :])
        n_ctx_blocks = len(blocks)
        if with_lat:
            blocks += lane_blocks(jnp.where(in_band, sw_ref[rows, :], NEG_INF))
        sink = jnp.full((tq, LANES), sink_ref[h * n_stack + g] * LOG2_E, F32)
        m = jnp.maximum(sink, jnp.max(functools.reduce(jnp.maximum, blocks), axis=-1, keepdims=True))
        p = [jnp.exp2(b - m) for b in blocks]
        denom = jnp.sum(functools.reduce(jnp.add, p), axis=-1, keepdims=True) + jnp.exp2(sink - m)
        linv_ref[rows, :] = 1.0 / denom
        pc_ref[rows, :] = jnp.concatenate(p[:n_ctx_blocks], axis=1).astype(BF16)
        if with_lat:
            pw_ref[rows, :] = jnp.concatenate(p[n_ctx_blocks:], axis=1).astype(BF16)

    o = _dot(pc_ref[...], vc_ref[...])
    if with_lat:
        o = o + _dot(pw_ref[...], vl_ref[pl.ds(start, WIN_SPAN), :])
    o = o * linv_ref[...]
    for g in range(n_stack):
        o_ref[:, g * HEAD_DIM:(g + 1) * HEAD_DIM] = o[g * tq:(g + 1) * tq].astype(BF16)


def _attn_window(q_src, qkv_ctx, qkv_lat, sink, *, n_q, n_ctx, n_lat, batch):
    with_lat = qkv_lat is not None
    tq = WIN_TQ
    q_tiles = n_q // tq
    n_stack = B_HEADS // B_KV_HEADS
    q_w = n_stack * HEAD_DIM
    q_col = B_Q * LANES // q_w
    in_specs = [
        pl.BlockSpec(memory_space=pltpu.SMEM),
        pl.BlockSpec((tq, q_w), lambda b, h, t: (b * q_tiles + t, q_col + h)),
        pl.BlockSpec((n_ctx, HEAD_DIM), lambda b, h, t: (b, B_K + h)),
        pl.BlockSpec((n_ctx, HEAD_DIM), lambda b, h, t: (b, B_V + h)),
    ]
    args = [sink, q_src, qkv_ctx, qkv_ctx]
    if with_lat:
        in_specs += [
            pl.BlockSpec((n_lat, HEAD_DIM), lambda b, h, t: (b, B_K + h)),
            pl.BlockSpec((n_lat, HEAD_DIM), lambda b, h, t: (b, B_V + h)),
        ]
        args += [qkv_lat, qkv_lat]
    rows = n_stack * tq
    scratch = [
        pltpu.VMEM((rows, HEAD_DIM), BF16),
        pltpu.VMEM((rows, n_ctx), F32),
        pltpu.VMEM((rows, n_ctx), BF16),
        pltpu.VMEM((rows, LANES), F32),
    ]
    if with_lat:
        scratch += [pltpu.VMEM((rows, WIN_SPAN), F32), pltpu.VMEM((rows, WIN_SPAN), BF16)]
    return pl.pallas_call(
        functools.partial(_attn_window_kernel, with_lat=with_lat, n_lat=n_lat),
        grid=(batch, B_KV_HEADS, q_tiles),
        in_specs=in_specs,
        out_specs=pl.BlockSpec((tq, q_w), lambda b, h, t: (b * q_tiles + t, h)),
        out_shape=jax.ShapeDtypeStruct((batch * n_q, BRANCH_WIDTH), BF16),
        scratch_shapes=scratch,
        name="attn_window",
        compiler_params=_params(("arbitrary", "arbitrary", "arbitrary")),
    )(*args)


MERGE_TN = 1024


def _merge_kernel(h_ref, oa_ref, ob_ref, oc_ref, wg_ref, wb_ref, y_ref, acc_ref):
    r = pl.program_id(2)
    gate = jax.nn.sigmoid(_dot(h_ref[...], wg_ref[...]))
    for k, o_ref in enumerate((oa_ref, ob_ref, oc_ref)):
        @pl.when(r == k)
        def _(o_ref=o_ref, k=k):
            part = gate * _dot(o_ref[...], wb_ref[...])
            if k == 0:
                acc_ref[...] = part
            elif k < N_BRANCH - 1:
                acc_ref[...] += part
            else:
                y_ref[...] = (acc_ref[...] + part).astype(BF16)


def _merge(h, o_a, o_b, o_c, w_in, w_branch, *, layer, tm):
    rows = h.shape[0]
    n_col = D_MODEL // MERGE_TN
    gate_col = QKV_WIDTH // MERGE_TN
    o_spec = pl.BlockSpec((tm, BRANCH_WIDTH), lambda i, n, r: (i, 0))
    return pl.pallas_call(
        _merge_kernel,
        grid=(rows // tm, n_col, N_BRANCH),
        in_specs=[
            pl.BlockSpec((tm, D_MODEL), lambda i, n, r: (i, 0)),
            o_spec, o_spec, o_spec,
            pl.BlockSpec((None, D_MODEL, MERGE_TN), lambda i, n, r: (layer, 0, gate_col + r * n_col + n)),
            pl.BlockSpec((None, None, BRANCH_WIDTH, MERGE_TN), lambda i, n, r: (layer, r, 0, n)),
        ],
        out_specs=pl.BlockSpec((tm, MERGE_TN), lambda i, n, r: (i, n)),
        out_shape=jax.ShapeDtypeStruct((rows, D_MODEL), BF16),
        scratch_shapes=[pltpu.VMEM((tm, MERGE_TN), F32)],
        name="branch_merge",
        compiler_params=_params(("arbitrary", "arbitrary", "arbitrary")),
    )(h, o_a, o_b, o_c, w_in, w_branch)


def _out_kernel(y_ref, x_ref, mod_ref, gpost_ref, w_ref, o_ref):
    z = _dot(y_ref[...], w_ref[...])
    o_ref[...] = x_ref[...] + mod_ref[5:6, :] * _rms(z, gpost_ref[...])


def _out_project(y, x, mod, g_post, w_out, *, layer, tm, rows_per_group):
    rows = x.shape[0]
    tiles_per_group = rows_per_group // tm
    return pl.pallas_call(
        _out_kernel,
        grid=(rows // tm,),
        in_specs=[
            pl.BlockSpec((tm, D_MODEL), lambda i: (i, 0)),
            pl.BlockSpec((tm, D_MODEL), lambda i: (i, 0)),
            pl.BlockSpec((None, N_MOD, D_MODEL), lambda i: (i // tiles_per_group, 0, 0)),
            pl.BlockSpec((1, D_MODEL), lambda i: (0, 0)),
            pl.BlockSpec((None, D_MODEL, D_MODEL), lambda i: (layer, 0, 0)),
        ],
        out_specs=pl.BlockSpec((tm, D_MODEL), lambda i: (i, 0)),
        out_shape=jax.ShapeDtypeStruct((rows, D_MODEL), F32),
        name="out_projection",
        compiler_params=_params(("arbitrary",)),
    )(y, x, mod, g_post, w_out)


def kernel(x, c, ctx, c_ctx, w_ada, b_ada, norm_pre, norm_post, w_ff_in, w_ff_out, w_in, qk_norm_a,
           sink_b, lam_c, subln_c, w_branch, w_out):
    batch, n_lat, _ = x.shape
    n_ctx = ctx.shape[1]
    depth = w_ada.shape[0]
    assert (depth, x.shape[2]) == (DEPTH, D_MODEL)

    tm_lat_ffn, tm_lat_proj, tm_lat_merge, tm_lat_out = 512, 1024, 1024, 512
    tm_ctx = batch * n_ctx

    cvec = jnp.concatenate([c, c_ctx[None, :]], axis=0)
    mod_all = _modulation(cvec, w_ada, b_ada)
    tables = _rope_tables(n_lat)

    w_ff_in_h = w_ff_in.astype(BF16)
    w_ff_out_h = w_ff_out.astype(BF16)
    w_in_h = w_in.astype(BF16)
    w_branch_h = w_branch.astype(BF16)
    w_out_h = w_out.astype(BF16)

    x_lat = x.reshape(batch * n_lat, D_MODEL)
    x_ctx = ctx.reshape(batch * n_ctx, D_MODEL)

    for l in range(depth):
        last = l == depth - 1
        lambda_init = 0.8 - 0.6 * math.exp(-0.3 * l)
        mod_lat = mod_all[l, 0:batch].reshape(batch, N_MOD, D_MODEL)
        mod_ctx = mod_all[l, batch:batch + 1].reshape(1, N_MOD, D_MODEL)
        g_pre = [norm_pre[l, s][None, :] for s in range(3)]
        g_post = [norm_post[l, s][None, :] for s in range(3)]
        lat = dict(rows_per_group=n_lat)
        cx = dict(rows_per_group=tm_ctx)

        ffn1 = dict(layer=l, which=0, mod_row=0)
        ffn2 = dict(layer=l, which=1, mod_row=6)
        x_ctx = _ffn_half_step(x_ctx, mod_ctx, g_pre[0], g_post[0], w_ff_in_h, w_ff_out_h,
                               tm=tm_ctx, **ffn1, **cx)
        x_lat = _ffn_half_step(x_lat, mod_lat, g_pre[0], g_post[0], w_ff_in_h, w_ff_out_h,
                               tm=tm_lat_ffn, **ffn1, **lat)

        qkv_ctx, h_ctx = _project(x_ctx, mod_ctx, g_pre[1], w_in_h, qk_norm_a[l], None,
                                  layer=l, tm=tm_ctx, n_pos=n_lat, **cx)
        qkv_lat, h_lat = _project(x_lat, mod_lat, g_pre[1], w_in_h, qk_norm_a[l], tables,
                                  layer=l, tm=tm_lat_proj, n_pos=n_lat, **lat)

        dims = dict(n_ctx=n_ctx, n_lat=n_lat, batch=batch)
        diff_args = dict(lam=lam_c[l], subln=subln_c[l][None, :], lambda_init=lambda_init)
        a_lat = _attn_global(qkv_lat, qkv_ctx, qkv_lat, diff=False, tq=256, n_q=n_lat, **dims)
        b_lat = _attn_window(qkv_lat, qkv_ctx, qkv_lat, sink_b[l], n_q=n_lat, **dims)
        c_lat = _attn_global(qkv_lat, qkv_ctx, qkv_lat, diff=True, tq=512, n_q=n_lat, **dims, **diff_args)
        y_lat = _merge(h_lat, a_lat, b_lat, c_lat, w_in_h, w_branch_h, layer=l, tm=tm_lat_merge)
        x_lat = _out_project(y_lat, x_lat, mod_lat, g_post[1], w_out_h, layer=l, tm=tm_lat_out, **lat)
        x_lat = _ffn_half_step(x_lat, mod_lat, g_pre[2], g_post[2], w_ff_in_h, w_ff_out_h,
                               tm=tm_lat_ffn, **ffn2, **lat)

        if not last:
            a_ctx = _attn_global(qkv_ctx, qkv_ctx, None, diff=False, tq=n_ctx, n_q=n_ctx, **dims)
            b_ctx = _attn_window(qkv_ctx, qkv_ctx, None, sink_b[l], n_q=n_ctx, **dims)
            c_ctx_o = _attn_global(qkv_ctx, qkv_ctx, None, diff=True, tq=n_ctx, n_q=n_ctx, **dims, **diff_args)
            y_ctx = _merge(h_ctx, a_ctx, b_ctx, c_ctx_o, w_in_h, w_branch_h, layer=l, tm=tm_ctx)
            x_ctx = _out_project(y_ctx, x_ctx, mod_ctx, g_post[1], w_out_h, layer=l, tm=tm_ctx, **cx)
            x_ctx = _ffn_half_step(x_ctx, mod_ctx, g_pre[2], g_post[2], w_ff_in_h, w_ff_out_h,
                                   tm=tm_ctx, **ffn2, **cx)

    return x_lat.reshape(batch, n_lat, D_MODEL)
```
